```python
import math
import jax
import jax.numpy as jnp
from jax import lax
import numpy as np

D_MODEL = 1024
BATCH = 8
SEQ = 8192
DEPTH = 4

N_MIXERS = 4
N_MEM = 256
EPS = 1e-6
ROPE_THETA = 10000.0
POS_OFFSET_MAX = 4096
CHUNK = 64
D_MIX = 768
XA_HEADS = 4
XA_HEAD_DIM = 64
D_XA = XA_HEADS * XA_HEAD_DIM
GLA_HEADS = 4
GLA_DK = D_MIX // 2 // GLA_HEADS
GLA_DV = D_MIX // GLA_HEADS
GLA_RANK = 16
GLA_GATE_NORM = 16.0
DIL_GROUPS = ((128, 1), (512, 4), (2048, 16))
DIL_HEADS = 4
DIL_HEAD_DIM = 128
DIL_BLOCK = 128
D_DIL = DIL_HEADS * DIL_HEAD_DIM
SSM_HEAD_DIM = 64
SSM_HEADS = D_MIX // SSM_HEAD_DIM
SSM_GROUPS = 2
SSM_STATE = 128
SSM_CONV = 4
HGRN_EXPAND = 128
HGRN_HEADS = D_MIX // HGRN_EXPAND
HGRN_DK = HGRN_EXPAND
HGRN_DV = D_MIX // HGRN_HEADS
D_FF = 2816
FFN_CONV = 3

kernel_name = "hybrid_interleaved_gla_dilated_ssd_hgrn2_block"

F32 = jnp.float32


def rms_norm(x, g):
    xf = x.astype(F32)
    y = xf * lax.rsqrt(jnp.mean(xf * xf, axis=-1, keepdims=True) + EPS)
    return (y * g.astype(F32)).astype(x.dtype)


def split_cols(y, sizes):
    return jnp.split(y, [int(c) for c in np.cumsum(sizes)[:-1]], axis=-1)


def causal_dwconv(x, w, b):
    K, S = w.shape[0], x.shape[1]
    xp = jnp.pad(x, ((0, 0), (K - 1, 0), (0, 0)))
    y = b
    for j in range(K):
        y = y + xp[:, j:j + S] * w[j]
    return y


def rope_tables(positions, dim):
    half = dim // 2
    inv_freq = ROPE_THETA ** (-jnp.arange(half, dtype=F32) / half)
    ang = positions.astype(F32)[..., None] * inv_freq
    return jnp.cos(ang)[:, :, None], jnp.sin(ang)[:, :, None]


def apply_rope(x, cos, sin):
    half = x.shape[-1] // 2
    x1, x2 = x[..., :half].astype(F32), x[..., half:].astype(F32)
    return jnp.concatenate([x1 * cos - x2 * sin, x2 * cos + x1 * sin], axis=-1).astype(x.dtype)


def chunked_gla(q, k, v, log_a):
    Bsz, S, H, K = q.shape
    V = v.shape[-1]
    C = CHUNK
    n = S // C
    q, k, v, log_a = (t.astype(F32).reshape(Bsz, n, C, H, -1) for t in (q, k, v, log_a))
    b = jnp.cumsum(log_a, axis=2)
    b_last = b[:, :, -1:]
    b_ref = b[:, :, C // 2 - 1:C // 2]
    att = jnp.einsum('bnihk,bnjhk->bnhij', q * jnp.exp(b - b_ref), k * jnp.exp(b_ref - b))
    att = jnp.where(jnp.tril(jnp.ones((C, C), bool)), att, 0.0)
    o_intra = jnp.einsum('bnhij,bnjhv->bnihv', att, v)
    q_inter = q * jnp.exp(b)
    k_state = k * jnp.exp(b_last - b)
    decay = jnp.exp(b_last[:, :, 0])

    def step(state, xs):
        qc, kc, vc, dc = xs
        o = jnp.einsum('bchk,bhkv->bchv', qc, state)
        state = dc[..., None] * state + jnp.einsum('bchk,bchv->bhkv', kc, vc)
        return state, o

    tm = lambda t: jnp.moveaxis(t, 1, 0)
    _, o_inter = lax.scan(step, jnp.zeros((Bsz, H, K, V), F32),
                          (tm(q_inter), tm(k_state), tm(v), tm(decay)))
    return (o_intra + jnp.moveaxis(o_inter, 0, 1)).reshape(Bsz, S, H, V)


def ssd_chunked(x, dt, A, Bm, Cm):
    Bsz, S, G, Hg, P = x.shape
    N = Bm.shape[-1]
    C = CHUNK
    n = S // C
    a = (dt * A).reshape(Bsz, n, C, G, Hg)
    xdt = (x * dt[..., None]).reshape(Bsz, n, C, G, Hg, P)
    Bc = Bm.astype(F32).reshape(Bsz, n, C, G, N)
    Cc = Cm.astype(F32).reshape(Bsz, n, C, G, N)
    acs = jnp.cumsum(a, axis=2)
    seg = acs[:, :, :, None] - acs[:, :, None]
    causal = jnp.tril(jnp.ones((C, C), bool))[:, :, None, None]
    Lmat = jnp.exp(jnp.where(causal, seg, -jnp.inf))
    cb = jnp.einsum('bnlgk,bnsgk->bnlsg', Cc, Bc)
    y_diag = jnp.einsum('bnlsgh,bnsghp->bnlghp', cb[..., None] * Lmat, xdt)
    x_end = xdt * jnp.exp(acs[:, :, -1:] - acs)[..., None]
    in_decay = jnp.exp(acs)
    chunk_decay = jnp.exp(acs[:, :, -1])

    def step(hst, xs):
        c_c, b_c, xe_c, ind_c, cd_c = xs
        y_off = jnp.einsum('blgk,bghpk->blghp', c_c, hst) * ind_c[..., None]
        hst = cd_c[..., None, None] * hst + jnp.einsum('bsgk,bsghp->bghpk', b_c, xe_c)
        return hst, y_off

    tm = lambda t: jnp.moveaxis(t, 1, 0)
    _, y_off = lax.scan(step, jnp.zeros((Bsz, G, Hg, P, N), F32),
                        (tm(Cc), tm(Bc), tm(x_end), tm(in_decay), tm(chunk_decay)))
    return (y_diag + jnp.moveaxis(y_off, 0, 1)).reshape(Bsz, S, G, Hg, P)


def dilated_window_attention(q, k, v, window, dilation):
    Bsz, S, H, hd = q.shape
    r = dilation
    W = window // dilation
    Q = DIL_BLOCK
    L = S // r
    nb = -(-L // Q)
    Lp = nb * Q

    def to_blocks(t):
        t = t.reshape(Bsz, L, r, H, hd).transpose(0, 2, 1, 3, 4)
        t = jnp.pad(t, ((0, 0), (0, 0), (0, Lp - L), (0, 0), (0, 0)))
        return t.reshape(Bsz, r, nb, Q, H, hd)

    def with_prev(t):
        prev = jnp.pad(t, ((0, 0), (0, 0), (1, 0), (0, 0), (0, 0), (0, 0)))[:, :, :-1]
        return jnp.concatenate([prev, t], axis=3)

    qb = to_blocks(q)
    kk = with_prev(to_blocks(k))
    vv = with_prev(to_blocks(v))
    i = jnp.arange(Q)[:, None]
    j = jnp.arange(2 * Q)[None, :]
    dist = Q + i - j
    band = (dist >= 0) & (dist <= W)
    valid = (jnp.arange(nb)[:, None, None] > 0) | (j >= Q)[None]
    mask = band[None] & valid
    s = jnp.einsum('brnqhd,brnkhd->brnhqk', qb, kk).astype(F32) * (hd ** -0.5)
    s = jnp.where(mask[None, None, :, None], s, -jnp.inf)
    m = jnp.max(s, axis=-1, keepdims=True)
    p = jnp.exp(s - m)
    l = jnp.sum(p, axis=-1, keepdims=True)
    o = jnp.einsum('brnhqk,brnkhd->brnqhd', p / l, vv.astype(F32))
    lse = (m + jnp.log(l))[..., 0].transpose(0, 1, 2, 4, 3)
    o = o.reshape(Bsz, r, Lp, H, hd)[:, :, :L].transpose(0, 2, 1, 3, 4).reshape(Bsz, S, H, hd)
    lse = lse.reshape(Bsz, r, Lp, H)[:, :, :L].transpose(0, 2, 1, 3).reshape(Bsz, S, H)
    return o, lse


def gla_mixer(h, w_in, w_gate2, b_gate, o_norm):
    Bsz, S, _ = h.shape
    q, k, v, glr, og, xq = split_cols(
        h @ w_in, [GLA_HEADS * GLA_DK, GLA_HEADS * GLA_DK, D_MIX, GLA_RANK, D_MIX, D_XA])
    log_a = jax.nn.log_sigmoid((glr @ w_gate2 + b_gate).astype(F32)) / GLA_GATE_NORM
    hs = lambda t: t.reshape(Bsz, S, GLA_HEADS, -1)
    o = chunked_gla(hs(q) * (GLA_DK ** -0.5), hs(k), hs(v), hs(log_a))
    o = rms_norm(o, o_norm).reshape(Bsz, S, D_MIX) * jax.nn.silu(og.astype(F32))
    return o.astype(h.dtype), xq


def dilated_mixer(h, w_in, q_norm, k_norm, positions):
    Bsz, S, _ = h.shape
    nh = len(DIL_GROUPS) * DIL_HEADS
    q, k, v, xq = split_cols(h @ w_in, [nh * DIL_HEAD_DIM] * 3 + [D_XA])
    hs = lambda t: t.reshape(Bsz, S, nh, DIL_HEAD_DIM)
    cos, sin = rope_tables(positions, DIL_HEAD_DIM)
    q = apply_rope(rms_norm(hs(q), q_norm), cos, sin)
    k = apply_rope(rms_norm(hs(k), k_norm), cos, sin)
    v = hs(v)
    outs, lses = [], []
    for g, (window, dilation) in enumerate(DIL_GROUPS):
        sl = slice(g * DIL_HEADS, (g + 1) * DIL_HEADS)
        o, lse = dilated_window_attention(q[:, :, sl], k[:, :, sl], v[:, :, sl], window, dilation)
        outs.append(o)
        lses.append(lse)
    wts = jax.nn.softmax(jnp.stack(lses), axis=0)
    o = jnp.sum(wts[..., None] * jnp.stack(outs), axis=0)
    return o.reshape(Bsz, S, D_DIL).astype(h.dtype), xq


def mamba2_mixer(h, w_in, conv_w, conv_b, dt_bias, a_log, d_skip, norm_g):
    Bsz, S, _ = h.shape
    GN = SSM_GROUPS * SSM_STATE
    hg = SSM_HEADS // SSM_GROUPS
    z, xbc, dt, xq = split_cols(h @ w_in, [D_MIX, D_MIX + 2 * GN, SSM_HEADS, D_XA])
    xbc = jax.nn.silu(causal_dwconv(xbc, conv_w, conv_b))
    xs, bm, cm = split_cols(xbc, [D_MIX, GN, GN])
    dt = jax.nn.softplus(dt.astype(F32) + dt_bias.astype(F32)).reshape(Bsz, S, SSM_GROUPS, hg)
    A = -jnp.exp(a_log.astype(F32)).reshape(SSM_GROUPS, hg)
    xs = xs.astype(F32).reshape(Bsz, S, SSM_GROUPS, hg, SSM_HEAD_DIM)
    y = ssd_chunked(xs, dt, A, bm.reshape(Bsz, S, SSM_GROUPS, SSM_STATE),
                    cm.reshape(Bsz, S, SSM_GROUPS, SSM_STATE))
    y = y + d_skip.astype(F32).reshape(SSM_GROUPS, hg)[..., None] * xs
    y = y.reshape(Bsz, S, D_MIX) * jax.nn.silu(z.astype(F32))
    y = rms_norm(y.reshape(Bsz, S, SSM_GROUPS, -1), norm_g.reshape(SSM_GROUPS, -1))
    return y.reshape(Bsz, S, D_MIX).astype(h.dtype), xq


def hgrn2_mixer(h, w_in, lower_bounds, o_norm, layer):
    Bsz, S, _ = h.shape
    q, f, i, og, xq = split_cols(
        h @ w_in, [HGRN_HEADS * HGRN_DK, HGRN_HEADS * HGRN_DK, D_MIX, D_MIX, D_XA])
    lbs = jnp.cumsum(jax.nn.softmax(lower_bounds.astype(F32), axis=0), axis=0)
    lb = lbs[layer] - lbs[0]
    fg = lb + (1.0 - lb) * jax.nn.sigmoid(f.astype(F32))
    hk = lambda t: t.reshape(Bsz, S, HGRN_HEADS, HGRN_DK)
    o = chunked_gla(hk(jax.nn.silu(q.astype(F32))), hk(1.0 - fg),
                    i.reshape(Bsz, S, HGRN_HEADS, HGRN_DV), hk(jnp.log(fg)))
    o = rms_norm(o, o_norm).reshape(Bsz, S, D_MIX) * jax.nn.sigmoid(og.astype(F32))
    return o.astype(h.dtype), xq


def memory_cross_attention(xq, mem_n, w_kv, q_norm, k_norm):
    Bsz, S, _ = xq.shape
    M = mem_n.shape[1]
    k, v = jnp.split(mem_n @ w_kv, 2, axis=-1)
    q = rms_norm(xq.reshape(Bsz, S, XA_HEADS, XA_HEAD_DIM), q_norm).astype(F32)
    k = rms_norm(k.reshape(Bsz, M, XA_HEADS, XA_HEAD_DIM), k_norm).astype(F32)
    v = v.reshape(Bsz, M, XA_HEADS, XA_HEAD_DIM).astype(F32)
    p = jax.nn.softmax(jnp.einsum('bshd,bmhd->bhsm', q, k) * (XA_HEAD_DIM ** -0.5), axis=-1)
    o = jnp.einsum('bhsm,bmhd->bshd', p, v)
    return o.reshape(Bsz, S, D_XA).astype(xq.dtype)


def conv_ffn(h, w_up, conv_w, conv_b, w_down):
    u = causal_dwconv(h @ w_up, conv_w, conv_b)
    gate, val = jnp.split(u, 2, axis=-1)
    return (jax.nn.silu(gate) * val) @ w_down


def setup_inputs(seed: int = 0) -> dict:
    key = jax.random.key(seed)
    ks = iter(jax.random.split(key, 40))

    def nrm(shape, scale=1.0):
        return jax.random.normal(next(ks), shape, F32) * scale

    def gain(shape):
        return 1.0 + nrm(shape, 0.02)

    D, F = D_MODEL, D_FF
    GN = SSM_GROUPS * SSM_STATE
    a_cols = 2 * GLA_HEADS * GLA_DK + 2 * D_MIX + GLA_RANK + D_XA
    b_cols = 3 * len(DIL_GROUPS) * D_DIL + D_XA
    c_cols = 2 * D_MIX + 2 * GN + SSM_HEADS + D_XA
    d_cols = 2 * HGRN_HEADS * HGRN_DK + 2 * D_MIX + D_XA
    a_out_in, b_out_in = D_MIX + D_XA, D_DIL + D_XA
    c_out_in, d_out_in = D_MIX + D_XA, D_MIX + D_XA
    inp = {}
    inp['x'] = nrm((BATCH, SEQ, D))
    inp['mem'] = nrm((BATCH, N_MEM, D))
    inp['positions'] = (jnp.arange(SEQ, dtype=jnp.int32)[None, :]
                        + jax.random.randint(next(ks), (BATCH, 1), 0, POS_OFFSET_MAX, dtype=jnp.int32))
    inp['mem_norm'] = gain((D,))
    inp['mix_norm'] = gain((DEPTH, D))
    inp['xa_w_kv'] = nrm((DEPTH, D, 2 * D_XA), D ** -0.5)
    inp['xa_q_norm'] = gain((DEPTH, XA_HEAD_DIM))
    inp['xa_k_norm'] = gain((DEPTH, XA_HEAD_DIM))
    inp['ffn_norm'] = gain((DEPTH, D))
    inp['ffn_w_up'] = nrm((DEPTH, D, 2 * F), D ** -0.5)
    inp['ffn_conv_w'] = nrm((DEPTH, FFN_CONV, 2 * F), FFN_CONV ** -0.5)
    inp['ffn_conv_b'] = nrm((DEPTH, 2 * F), 0.02)
    inp['ffn_w_down'] = nrm((DEPTH, F, D), 0.5 * F ** -0.5)
    inp['a_w_in'] = nrm((D, a_cols), D ** -0.5)
    inp['a_w_gate2'] = nrm((GLA_RANK, GLA_HEADS * GLA_DK), GLA_RANK ** -0.5)
    inp['a_b_gate'] = nrm((GLA_HEADS * GLA_DK,), 0.1)
    inp['a_o_norm'] = gain((GLA_DV,))
    inp['a_w_out'] = nrm((a_out_in, D), 0.5 * a_out_in ** -0.5)
    inp['b_w_in'] = nrm((D, b_cols), D ** -0.5)
    inp['b_q_norm'] = gain((DIL_HEAD_DIM,))
    inp['b_k_norm'] = gain((DIL_HEAD_DIM,))
    inp['b_w_out'] = nrm((b_out_in, D), 0.5 * b_out_in ** -0.5)
    inp['c_w_in'] = nrm((D, c_cols), D ** -0.5)
    inp['c_conv_w'] = nrm((SSM_CONV, D_MIX + 2 * GN), 0.5)
    inp['c_conv_b'] = nrm((D_MIX + 2 * GN,), 0.02)
    dt0 = jnp.exp(jax.random.uniform(next(ks), (SSM_HEADS,), F32, math.log(1e-3), math.log(1e-1)))
    inp['c_dt_bias'] = dt0 + jnp.log(-jnp.expm1(-dt0))
    inp['c_a_log'] = jnp.log(jax.random.uniform(next(ks), (SSM_HEADS,), F32, 1.0, 16.0))
    inp['c_d'] = gain((SSM_HEADS,))
    inp['c_norm'] = gain((D_MIX,))
    inp['c_w_out'] = nrm((c_out_in, D), 0.5 * c_out_in ** -0.5)
    inp['d_w_in'] = nrm((D, d_cols), D ** -0.5)
    inp['d_lower_bounds'] = nrm((DEPTH, HGRN_HEADS * HGRN_DK), 0.02)
    inp['d_o_norm'] = gain((HGRN_DV,))
    inp['d_w_out'] = nrm((d_out_in, D), 0.5 * d_out_in ** -0.5)
    return inp


def reference(x, mem, positions, mem_norm, mix_norm, xa_w_kv, xa_q_norm, xa_k_norm,
              ffn_norm, ffn_w_up, ffn_conv_w, ffn_conv_b, ffn_w_down,
              a_w_in, a_w_gate2, a_b_gate, a_o_norm, a_w_out,
              b_w_in, b_q_norm, b_k_norm, b_w_out,
              c_w_in, c_conv_w, c_conv_b, c_dt_bias, c_a_log, c_d, c_norm, c_w_out,
              d_w_in, d_lower_bounds, d_o_norm, d_w_out):
    mem_n = rms_norm(mem, mem_norm)
    for i in range(DEPTH):
        h = rms_norm(x, mix_norm[i])
        kind = i % N_MIXERS
        if kind == 0:
            tok, xq = gla_mixer(h, a_w_in, a_w_gate2, a_b_gate, a_o_norm)
            w_out = a_w_out
        elif kind == 1:
            tok, xq = dilated_mixer(h, b_w_in, b_q_norm, b_k_norm, positions)
            w_out = b_w_out
        elif kind == 2:
            tok, xq = mamba2_mixer(h, c_w_in, c_conv_w, c_conv_b, c_dt_bias, c_a_log, c_d, c_norm)
            w_out = c_w_out
        else:
            tok, xq = hgrn2_mixer(h, d_w_in, d_lower_bounds, d_o_norm, i)
            w_out = d_w_out
        xa = memory_cross_attention(xq, mem_n, xa_w_kv[i], xa_q_norm[i], xa_k_norm[i])
        x = x + jnp.concatenate([tok, xa], axis=-1) @ w_out
        x = x + conv_ffn(rms_norm(x, ffn_norm[i]), ffn_w_up[i], ffn_conv_w[i], ffn_conv_b[i],
                         ffn_w_down[i])
    return x
```

```python
import functools
import math

import jax
import jax.numpy as jnp
from jax import lax
from jax.experimental import pallas as pl
from jax.experimental.pallas import tpu as pltpu

F32 = jnp.float32
BF16 = jnp.bfloat16

LANE = 128
SUBLANE = 8
VMEM_LIMIT = 56 * 1024 * 1024

EPS = 1e-6
ROPE_THETA = 10000.0
CHUNK = 64
N_MIXERS = 4
XA_HEADS, XA_HEAD_DIM = 4, 64
GLA_HEADS, GLA_RANK, GLA_GATE_NORM = 4, 16, 16.0
DIL_GROUPS = ((128, 1), (512, 4), (2048, 16))
DIL_HEADS, DIL_HEAD_DIM, DIL_BLOCK = 4, 128, 128
SSM_HEAD_DIM, SSM_GROUPS, SSM_STATE = 64, 2, 128
HGRN_EXPAND = 128

TOKEN_TILE = 512
SEQ_TILE = 512
COL_CHUNK = 512


def _params(*sem):
    return pltpu.CompilerParams(dimension_semantics=sem, vmem_limit_bytes=VMEM_LIMIT)


def _dot(a, b):
    return jnp.dot(a.astype(BF16), b.astype(BF16), preferred_element_type=F32)


def _dot_nt(a, b):
    return lax.dot_general(a.astype(BF16), b.astype(BF16), (((1,), (1,)), ((), ())),
                           preferred_element_type=F32)


def _dot_tn(a, b):
    return lax.dot_general(a.astype(BF16), b.astype(BF16), (((0,), (0,)), ((), ())),
                           preferred_element_type=F32)


def _split(x):
    hi = x.astype(BF16)
    return hi, (x - hi.astype(F32)).astype(BF16)


def _dot_exact_rhs(a, b01):
    hi, lo = _split(a)
    return (jnp.dot(hi, b01, preferred_element_type=F32)
            + jnp.dot(lo, b01, preferred_element_type=F32))


def _cumsum_rows(tri01, x):
    hi, lo = _split(x)
    return (jnp.dot(tri01, hi, preferred_element_type=F32)
            + jnp.dot(tri01, lo, preferred_element_type=F32))


def _dot_hilo(a, b):
    ah, al = _split(a)
    bh, bl = _split(b)
    return (jnp.dot(ah, bh, preferred_element_type=F32)
            + jnp.dot(ah, bl, preferred_element_type=F32)
            + jnp.dot(al, bh, preferred_element_type=F32))


def _sigmoid(x):
    return 1.0 / (1.0 + jnp.exp(-x))


def _silu(x):
    return x * _sigmoid(x)


def _softplus(x):
    return jnp.maximum(x, 0.0) + jnp.log1p(jnp.exp(-jnp.abs(x)))


def _log_sigmoid(x):
    return -_softplus(-x)


def _tril01(n):
    r = lax.broadcasted_iota(jnp.int32, (n, n), 0)
    c = lax.broadcasted_iota(jnp.int32, (n, n), 1)
    return r >= c


def _norm_matmul_kernel(x_ref, g_ref, w_ref, *o_refs, segs):
    x = x_ref[...]
    h = x * lax.rsqrt(jnp.mean(x * x, axis=-1, keepdims=True) + EPS)
    h = (h * g_ref[...]).astype(BF16)
    c0 = 0
    for o_ref, width in zip(o_refs, segs):
        for j in range(0, width, COL_CHUNK):
            w = min(COL_CHUNK, width - j)
            o_ref[:, j:j + w] = jnp.dot(h, w_ref[:, c0 + j:c0 + j + w],
                                        preferred_element_type=F32).astype(o_ref.dtype)
        c0 += width


def _norm_matmul(x, g, w, segs, dtypes, tm=TOKEN_TILE):
    M, D = x.shape
    tm = min(tm, M)
    assert M % tm == 0 and w.shape == (D, sum(segs)) and all(s % LANE == 0 for s in segs)
    return pl.pallas_call(
        functools.partial(_norm_matmul_kernel, segs=tuple(segs)),
        grid=(M // tm,),
        in_specs=[pl.BlockSpec((tm, D), lambda i: (i, 0)),
                  pl.BlockSpec((1, D), lambda i: (0, 0)),
                  pl.BlockSpec(w.shape, lambda i: (0, 0))],
        out_specs=[pl.BlockSpec((tm, s), lambda i: (i, 0)) for s in segs],
        out_shape=[jax.ShapeDtypeStruct((M, s), dt) for s, dt in zip(segs, dtypes)],
        compiler_params=_params("parallel"),
        name="norm_matmul",
    )(x, g.reshape(1, D).astype(F32), w)


def _gla_chunk_heads(q, k, v, la, tri01, tril, st_ref, H, KP, VP):
    C = q.shape[0]
    b = _cumsum_rows(tri01, la)
    outs = []
    for h in range(H):
        ks = slice(h * KP, (h + 1) * KP)
        bh = b[:, ks]
        bl = b[C - 1:C, ks]
        br = b[C // 2 - 1:C // 2, ks]
        qh, kh, vh = q[:, ks], k[:, ks], v[:, h * VP:(h + 1) * VP]
        att = _dot_nt(qh * jnp.exp(bh - br), kh * jnp.exp(br - bh))
        att = jnp.where(tril, att, 0.0)
        st = st_ref[h]
        o = _dot(att, vh) + _dot_nt(qh * jnp.exp(bh), st)
        st_ref[h] = jnp.exp(bl) * st + _dot_tn(vh, kh * jnp.exp(bl - bh))
        outs.append(o)
    return outs


def _head_rms_gate(o, g, gate, dv):
    y = o * lax.rsqrt(jnp.sum(o * o, axis=-1, keepdims=True) * (1.0 / dv) + EPS)
    return (y * g) * gate


def _gla_kernel(q_ref, k_ref, v_ref, og_ref, glr_ref, wg_ref, bg_ref, on_ref, o_ref, st_ref,
                *, H, KP, VP, dk, dv, T):
    @pl.when(pl.program_id(1) == 0)
    def _():
        st_ref[...] = jnp.zeros_like(st_ref)

    tril = _tril01(CHUNK)
    tri01 = tril.astype(BF16)
    wg = wg_ref[...]
    bg = bg_ref[...]
    on = on_ref[...]

    def chunk(c, carry):
        rows = pl.ds(pl.multiple_of(c * CHUNK, CHUNK), CHUNK)
        la = _log_sigmoid(_dot_hilo(glr_ref[rows, :], wg) + bg) * (1.0 / GLA_GATE_NORM)
        q = q_ref[rows, :] * (dk ** -0.5)
        outs = _gla_chunk_heads(q, k_ref[rows, :], v_ref[rows, :], la, tri01, tril, st_ref,
                                H, KP, VP)
        og = og_ref[rows, :]
        for h, o in enumerate(outs):
            vs = slice(h * VP, (h + 1) * VP)
            o_ref[rows, vs] = _head_rms_gate(o, on, _silu(og[:, vs]), dv).astype(o_ref.dtype)
        return carry

    lax.fori_loop(0, T // CHUNK, chunk, 0)


def _hgrn_kernel(q_ref, f_ref, v_ref, og_ref, lb_ref, on_ref, o_ref, st_ref, *, H, KP, VP, dv, T):
    @pl.when(pl.program_id(1) == 0)
    def _():
        st_ref[...] = jnp.zeros_like(st_ref)

    tril = _tril01(CHUNK)
    tri01 = tril.astype(BF16)
    lb = lb_ref[...]
    on = on_ref[...]

    def chunk(c, carry):
        rows = pl.ds(pl.multiple_of(c * CHUNK, CHUNK), CHUNK)
        fg = lb + (1.0 - lb) * _sigmoid(f_ref[rows, :])
        q = _silu(q_ref[rows, :])
        outs = _gla_chunk_heads(q, 1.0 - fg, v_ref[rows, :], jnp.log(fg), tri01, tril, st_ref,
                                H, KP, VP)
        og = og_ref[rows, :]
        for h, o in enumerate(outs):
            vs = slice(h * VP, (h + 1) * VP)
            o_ref[rows, vs] = _head_rms_gate(o, on, _sigmoid(og[:, vs]), dv).astype(o_ref.dtype)
        return carry

    lax.fori_loop(0, T // CHUNK, chunk, 0)


def _seq_spec(T, width, col=0):
    return pl.BlockSpec((None, T, width), lambda b, t: (b, t, col))


def _const_spec(shape):
    return pl.BlockSpec(shape, lambda b, t: (0,) * len(shape))


def _gla_mixer(q, k, v, og, glr, wg, bg, on, *, H, KP, VP, dk, dv):
    B, S, _ = q.shape
    T = min(SEQ_TILE, S)
    assert S % T == 0 and T % CHUNK == 0
    return pl.pallas_call(
        functools.partial(_gla_kernel, H=H, KP=KP, VP=VP, dk=dk, dv=dv, T=T),
        grid=(B, S // T),
        in_specs=[_seq_spec(T, H * KP), _seq_spec(T, H * KP), _seq_spec(T, H * VP),
                  _seq_spec(T, H * VP), _seq_spec(T, LANE),
                  _const_spec(wg.shape), _const_spec(bg.shape), _const_spec(on.shape)],
        out_specs=_seq_spec(T, H * VP),
        out_shape=jax.ShapeDtypeStruct((B, S, H * VP), BF16),
        scratch_shapes=[pltpu.VMEM((H, VP, KP), F32)],
        compiler_params=_params("parallel", "arbitrary"),
        name="gla_mixer",
    )(q, k, v, og, glr, wg, bg, on)


def _hgrn_mixer(q, f, v, og, lb, on, *, H, KP, VP, dv):
    B, S, _ = q.shape
    T = min(SEQ_TILE, S)
    assert S % T == 0 and T % CHUNK == 0
    return pl.pallas_call(
        functools.partial(_hgrn_kernel, H=H, KP=KP, VP=VP, dv=dv, T=T),
        grid=(B, S // T),
        in_specs=[_seq_spec(T, H * KP), _seq_spec(T, H * KP), _seq_spec(T, H * VP),
                  _seq_spec(T, H * VP), _const_spec(lb.shape), _const_spec(on.shape)],
        out_specs=_seq_spec(T, H * VP),
        out_shape=jax.ShapeDtypeStruct((B, S, H * VP), BF16),
        scratch_shapes=[pltpu.VMEM((H, VP, KP), F32)],
        compiler_params=_params("parallel", "arbitrary"),
        name="hgrn_mixer",
    )(q, f, v, og, lb, on)


def _ssd_kernel(z_ref, xbc_ref, dt_ref, cw_ref, cb_ref, dtb_ref, alog_ref, dsk_ref, ng_ref, e_ref,
                o_ref, buf_ref, xc_ref, st_ref, *, T, DM, N, G, KC):
    t = pl.program_id(1)
    HALO = SUBLANE

    @pl.when(t == 0)
    def _():
        st_ref[...] = jnp.zeros_like(st_ref)
        buf_ref[0:HALO, :] = jnp.zeros((HALO, buf_ref.shape[1]), F32)

    @pl.when(t > 0)
    def _():
        buf_ref[0:HALO, :] = buf_ref[T:T + HALO, :]

    buf_ref[HALO:HALO + T, :] = xbc_ref[...]

    C = CHUNK
    GW = DM // G
    tril = _tril01(C)
    tri01 = tril.astype(BF16)
    e01 = e_ref[...]
    a_neg = -jnp.exp(alog_ref[...])
    dtb = dtb_ref[...]
    cbias = cb_ref[...]
    dsk = dsk_ref[...]
    ng = ng_ref[...]
    P = SSM_HEAD_DIM
    rr = lax.broadcasted_iota(jnp.int32, (C, DM), 0)
    cc = lax.broadcasted_iota(jnp.int32, (C, DM), 1)
    diag_sel = (cc & (P - 1)) == rr
    r2 = lax.broadcasted_iota(jnp.int32, (C, 2 * P), 0)
    c2 = lax.broadcasted_iota(jnp.int32, (C, 2 * P), 1)
    causal2 = r2 >= (c2 & (P - 1))
    lo_half = lax.broadcasted_iota(jnp.int32, (C, 2 * P), 1) < P

    for r0 in range(0, T, C):
        acc = cbias
        for j in range(KC):
            lo = r0 + HALO - (KC - 1) + j
            acc = acc + cw_ref[j:j + 1, :] * buf_ref[lo:lo + C, :]
        xc_ref[r0:r0 + C, :] = _silu(acc)

    def chunk(c, carry):
        rows = pl.ds(pl.multiple_of(c * C, C), C)
        xc = xc_ref[rows, :]
        xs = xc[:, :DM]
        dt = _softplus(dt_ref[rows, :] + dtb)
        acs = _cumsum_rows(tri01, dt * a_neg)
        dt_e = _dot_exact_rhs(dt, e01)
        acs_e = _dot_exact_rhs(acs, e01)
        acs_row = jnp.sum(jnp.where(diag_sel, acs_e, 0.0), axis=0, keepdims=True)
        acs_last = acs_e[C - 1:C, :]
        xdt = xs * dt_e
        z = z_ref[rows, :]
        for g in range(G):
            gs = slice(g * GW, (g + 1) * GW)
            bm = xc[:, DM + g * N:DM + (g + 1) * N]
            cm = xc[:, DM + G * N + g * N:DM + G * N + (g + 1) * N]
            cb2 = _dot_nt(cm, jnp.concatenate([bm, bm], axis=0))
            yd = []
            for p in range(GW // (2 * P)):
                ps = slice(g * GW + p * 2 * P, g * GW + (p + 1) * 2 * P)
                seg = acs_e[:, ps] - acs_row[:, ps]
                lmat = jnp.exp(jnp.where(causal2, seg, -jnp.inf))
                xp = xdt[:, ps]
                rhs = jnp.concatenate([jnp.where(lo_half, xp, 0.0), jnp.where(lo_half, 0.0, xp)],
                                      axis=0)
                yd.append(_dot(cb2 * lmat, rhs))
            y = jnp.concatenate(yd, axis=1)
            st = st_ref[g]
            y = y + _dot(cm, st) * jnp.exp(acs_e[:, gs])
            xe = xdt[:, gs] * jnp.exp(acs_last[:, gs] - acs_e[:, gs])
            st_ref[g] = jnp.exp(acs_last[:, gs]) * st + _dot_tn(bm, xe)
            y = y + dsk[:, gs] * xs[:, gs]
            y = y * _silu(z[:, gs])
            y = y * lax.rsqrt(jnp.mean(y * y, axis=-1, keepdims=True) + EPS)
            o_ref[rows, gs] = (y * ng[:, gs]).astype(o_ref.dtype)
        return carry

    lax.fori_loop(0, T // C, chunk, 0)


def _ssd_mixer(z, xbc, dt, cw, cb, dtb, alog, dsk, ng, e01):
    B, S, DM = z.shape
    W = xbc.shape[-1]
    T = min(SEQ_TILE, S)
    assert S % T == 0 and T % CHUNK == 0
    G, N = SSM_GROUPS, SSM_STATE
    return pl.pallas_call(
        functools.partial(_ssd_kernel, T=T, DM=DM, N=N, G=G, KC=cw.shape[0]),
        grid=(B, S // T),
        in_specs=[_seq_spec(T, DM), _seq_spec(T, W), _seq_spec(T, LANE),
                  _const_spec(cw.shape), _const_spec(cb.shape), _const_spec(dtb.shape),
                  _const_spec(alog.shape), _const_spec(dsk.shape), _const_spec(ng.shape),
                  _const_spec(e01.shape)],
        out_specs=_seq_spec(T, DM),
        out_shape=jax.ShapeDtypeStruct((B, S, DM), BF16),
        scratch_shapes=[pltpu.VMEM((T + SUBLANE, W), F32), pltpu.VMEM((T, W), F32),
                        pltpu.VMEM((G, N, DM // G), F32)],
        compiler_params=_params("parallel", "arbitrary"),
        name="ssd_mixer",
    )(z, xbc, dt, cw, cb, dtb, alog, dsk, ng, e01)


def _qk_rope_kernel(q_ref, k_ref, pos_ref, qn_ref, kn_ref, invf_ref, sign_ref, qo_ref, ko_ref,
                    *, NH, HD):
    ang = pos_ref[...].astype(F32) * invf_ref[...]
    cos = jnp.cos(ang)
    sin = jnp.sin(ang) * sign_ref[...]
    for src, nrm, dst in ((q_ref, qn_ref, qo_ref), (k_ref, kn_ref, ko_ref)):
        g = nrm[...]
        for h in range(NH):
            hs = slice(h * HD, (h + 1) * HD)
            x = src[:, hs]
            y = (x * lax.rsqrt(jnp.mean(x * x, axis=-1, keepdims=True) + EPS)) * g
            dst[:, hs] = (y * cos + pltpu.roll(y, HD // 2, 1) * sin).astype(dst.dtype)


def _qk_rope(q, k, pos, qn, kn, tm=TOKEN_TILE):
    M, W = q.shape
    HD = DIL_HEAD_DIM
    tm = min(tm, M)
    half = HD // 2
    inv_freq = ROPE_THETA ** (-jnp.arange(half, dtype=F32) / half)
    invf = jnp.concatenate([inv_freq, inv_freq]).reshape(1, HD)
    sign = jnp.concatenate([-jnp.ones((half,), F32), jnp.ones((half,), F32)]).reshape(1, HD)
    row = lambda w: pl.BlockSpec((tm, w), lambda i: (i, 0))
    cst = lambda w: pl.BlockSpec((1, w), lambda i: (0, 0))
    return pl.pallas_call(
        functools.partial(_qk_rope_kernel, NH=W // HD, HD=HD),
        grid=(M // tm,),
        in_specs=[row(W), row(W), row(1), cst(HD), cst(HD), cst(HD), cst(HD)],
        out_specs=[row(W), row(W)],
        out_shape=[jax.ShapeDtypeStruct((M, W), BF16)] * 2,
        compiler_params=_params("parallel"),
        name="qk_norm_rope",
    )(q, k, pos, qn.reshape(1, HD), kn.reshape(1, HD), invf, sign)


def _dil_attn_kernel(q_ref, kp_ref, kc_ref, vp_ref, vc_ref, o_ref, lse_ref, *, nq, W, NH, HD):
    n0 = pl.program_id(2)
    Q = DIL_BLOCK
    i_idx = lax.broadcasted_iota(jnp.int32, (Q, 2 * Q), 0)
    j_idx = lax.broadcasted_iota(jnp.int32, (Q, 2 * Q), 1)
    dist = Q + i_idx - j_idx
    band = (dist >= 0) & (dist <= W)
    jmin = jnp.where(n0 > 0, 0, Q)
    band_first = band & (j_idx >= jmin)
    lane = lax.broadcasted_iota(jnp.int32, (Q, LANE), 1)
    scale = HD ** -0.5
    for i in range(nq):
        rows = slice(i * Q, (i + 1) * Q)
        mask = band_first if i == 0 else band
        lse_tile = jnp.zeros((Q, LANE), F32)
        for h in range(NH):
            hs = slice(h * HD, (h + 1) * HD)
            if i == 0:
                kk = jnp.concatenate([kp_ref[:, hs], kc_ref[0:Q, hs]], axis=0)
                vv = jnp.concatenate([vp_ref[:, hs], vc_ref[0:Q, hs]], axis=0)
            else:
                kk = kc_ref[(i - 1) * Q:(i + 1) * Q, hs]
                vv = vc_ref[(i - 1) * Q:(i + 1) * Q, hs]
            s = _dot_nt(q_ref[rows, hs], kk) * scale
            s = jnp.where(mask, s, -jnp.inf)
            m = jnp.max(s, axis=-1, keepdims=True)
            p = jnp.exp(s - m)
            l = jnp.sum(p, axis=-1, keepdims=True)
            o_ref[rows, hs] = _dot(p, vv) * (1.0 / l)
            lse_tile = jnp.where(lane == h, m + jnp.log(l), lse_tile)
        lse_ref[rows, :] = lse_tile


def _dil_attention(qr, kr, v, g, window, dilation):
    B, S, WALL = qr.shape
    NH, HD, Q = DIL_HEADS, DIL_HEAD_DIM, DIL_BLOCK
    GW = NH * HD
    NG = WALL // GW
    r = dilation
    W = window // dilation
    L = S // r
    assert S % r == 0 and L % Q == 0 and W <= Q
    nq = min(4, L // Q)
    assert (L // Q) % nq == 0
    view = lambda t: t.reshape(B, L, r * WALL)
    cur = pl.BlockSpec((None, nq * Q, GW), lambda b, rho, n: (b, n, rho * NG + g))
    prev = pl.BlockSpec((None, Q, GW),
                        lambda b, rho, n: (b, jnp.maximum(n * nq - 1, 0), rho * NG + g))
    o, lse = pl.pallas_call(
        functools.partial(_dil_attn_kernel, nq=nq, W=W, NH=NH, HD=HD),
        grid=(B, r, L // (nq * Q)),
        in_specs=[cur, prev, cur, prev, cur],
        out_specs=[pl.BlockSpec((None, nq * Q, GW), lambda b, rho, n: (b, n, rho)),
                   pl.BlockSpec((None, nq * Q, LANE), lambda b, rho, n: (b, n, rho))],
        out_shape=[jax.ShapeDtypeStruct((B, L, r * GW), F32),
                   jax.ShapeDtypeStruct((B, L, r * LANE), F32)],
        compiler_params=_params("parallel", "parallel", "arbitrary"),
        name=f"dilated_attention_r{r}",
    )(view(qr), view(kr), view(kr), view(v), view(v))
    return o.reshape(B * S, GW), lse.reshape(B * S, LANE)


def _dil_merge_kernel(*refs, NG, NH, HD):
    o_refs, l_refs, out_ref = refs[:NG], refs[NG:2 * NG], refs[2 * NG]
    ls = [r[...] for r in l_refs]
    m = functools.reduce(jnp.maximum, ls)
    es = [jnp.exp(l - m) for l in ls]
    inv = 1.0 / functools.reduce(lambda a, b: a + b, es)
    for h in range(NH):
        hs = slice(h * HD, (h + 1) * HD)
        acc = None
        for o_ref, e in zip(o_refs, es):
            term = (e[:, h:h + 1] * inv[:, h:h + 1]) * o_ref[:, hs]
            acc = term if acc is None else acc + term
        out_ref[:, hs] = acc.astype(out_ref.dtype)


def _dil_merge(outs, lses, tm=TOKEN_TILE):
    M, GW = outs[0].shape
    tm = min(tm, M)
    NG = len(outs)
    row = lambda w: pl.BlockSpec((tm, w), lambda i: (i, 0))
    return pl.pallas_call(
        functools.partial(_dil_merge_kernel, NG=NG, NH=DIL_HEADS, HD=DIL_HEAD_DIM),
        grid=(M // tm,),
        in_specs=[row(GW)] * NG + [row(LANE)] * NG,
        out_specs=row(GW),
        out_shape=jax.ShapeDtypeStruct((M, GW), BF16),
        compiler_params=_params("parallel"),
        name="dilated_merge",
    )(*outs, *lses)


def _xattn_kernel(xq_ref, kv_ref, qn_ref, kn_ref, o_ref, *, NH, HP, hd):
    qn = qn_ref[...]
    kn = kn_ref[...]
    scale = hd ** -0.5
    for h in range(NH):
        hs = slice(h * HP, (h + 1) * HP)
        q = xq_ref[:, hs]
        q = (q * lax.rsqrt(jnp.sum(q * q, axis=-1, keepdims=True) * (1.0 / hd) + EPS)) * qn
        k = kv_ref[:, hs]
        k = (k * lax.rsqrt(jnp.sum(k * k, axis=-1, keepdims=True) * (1.0 / hd) + EPS)) * kn
        v = kv_ref[:, NH * HP + h * HP:NH * HP + (h + 1) * HP]
        s = _dot_nt(q, k) * scale
        m = jnp.max(s, axis=-1, keepdims=True)
        p = jnp.exp(s - m)
        l = jnp.sum(p, axis=-1, keepdims=True)
        o_ref[:, hs] = (_dot(p, v) * (1.0 / l)).astype(o_ref.dtype)


def _xattn(xq, kv_all, layer, qn, kn):
    B, S, WQ = xq.shape
    Mm = kv_all.shape[1]
    T = min(SEQ_TILE, S)
    NH, HP = XA_HEADS, LANE
    return pl.pallas_call(
        functools.partial(_xattn_kernel, NH=NH, HP=HP, hd=XA_HEAD_DIM),
        grid=(B, S // T),
        in_specs=[_seq_spec(T, WQ),
                  pl.BlockSpec((None, Mm, 2 * WQ), lambda b, t: (b, 0, layer)),
                  _const_spec(qn.shape), _const_spec(kn.shape)],
        out_specs=_seq_spec(T, WQ),
        out_shape=jax.ShapeDtypeStruct((B, S, WQ), BF16),
        compiler_params=_params("parallel", "parallel"),
        name="memory_xattn",
    )(xq, kv_all, qn, kn)


def _out_proj_kernel(x_ref, tok_ref, xa_ref, wt_ref, wx_ref, o_ref):
    tok = tok_ref[...]
    xa = xa_ref[...]
    D = o_ref.shape[1]
    for j in range(0, D, COL_CHUNK):
        cs = slice(j, j + COL_CHUNK)
        o_ref[:, cs] = (x_ref[:, cs] + jnp.dot(tok, wt_ref[:, cs], preferred_element_type=F32)
                        + jnp.dot(xa, wx_ref[:, cs], preferred_element_type=F32))


def _out_proj(x, tok, xa, wt, wx, tm=TOKEN_TILE):
    M, D = x.shape
    tm = min(tm, M)
    row = lambda w: pl.BlockSpec((tm, w), lambda i: (i, 0))
    full = lambda a: pl.BlockSpec(a.shape, lambda i: (0, 0))
    return pl.pallas_call(
        _out_proj_kernel,
        grid=(M // tm,),
        in_specs=[row(D), row(tok.shape[1]), row(xa.shape[1]), full(wt), full(wx)],
        out_specs=row(D),
        out_shape=jax.ShapeDtypeStruct((M, D), F32),
        compiler_params=_params("parallel"),
        name="out_proj",
    )(x, tok, xa, wt, wx)


def _ffn_kernel(x_ref, g_ref, wu_ref, cw_ref, cb_ref, wd_ref, o_ref, carry_ref, buf_ref, acc_ref,
                *, T, FF, FC, KC):
    t = pl.program_id(1)
    HALO = SUBLANE

    @pl.when(t == 0)
    def _():
        carry_ref[...] = jnp.zeros_like(carry_ref)

    x = x_ref[...]
    h = x * lax.rsqrt(jnp.mean(x * x, axis=-1, keepdims=True) + EPS)
    h = (h * g_ref[...]).astype(BF16)

    def conv(cols):
        u = jnp.dot(h, wu_ref[:, cols], preferred_element_type=F32)
        buf_ref[0:HALO, :] = carry_ref[:, cols]
        buf_ref[HALO:HALO + T, :] = u
        carry_ref[:, cols] = u[T - HALO:T, :]
        acc = cb_ref[:, cols]
        for j in range(KC):
            acc = acc + cw_ref[j:j + 1, cols] * buf_ref[HALO - (KC - 1) + j:HALO - (KC - 1) + j + T, :]
        return acc

    for n, j in enumerate(range(0, FF, FC)):
        gate = conv(slice(j, j + FC))
        val = conv(slice(FF + j, FF + j + FC))
        act = (_silu(gate) * val).astype(BF16)
        d = jnp.dot(act, wd_ref[j:j + FC, :], preferred_element_type=F32)
        if n == 0:
            acc_ref[...] = d
        else:
            acc_ref[...] += d
    o_ref[...] = x + acc_ref[...]


def _ffn(x, g, wu, cw, cb, wd):
    B, S, D = x.shape
    FF = wd.shape[0]
    T = min(SEQ_TILE, S)
    FC = 256
    assert FF % FC == 0 and S % T == 0
    single = dict(pipeline_mode=pl.Buffered(1))
    return pl.pallas_call(
        functools.partial(_ffn_kernel, T=T, FF=FF, FC=FC, KC=cw.shape[0]),
        grid=(B, S // T),
        in_specs=[_seq_spec(T, D), _const_spec((1, D)),
                  pl.BlockSpec(wu.shape, lambda b, t: (0, 0), **single),
                  _const_spec(cw.shape), _const_spec(cb.shape),
                  pl.BlockSpec(wd.shape, lambda b, t: (0, 0), **single)],
        out_specs=_seq_spec(T, D),
        out_shape=jax.ShapeDtypeStruct((B, S, D), F32),
        scratch_shapes=[pltpu.VMEM((SUBLANE, 2 * FF), F32), pltpu.VMEM((T + SUBLANE, FC), F32),
                        pltpu.VMEM((T, D), F32)],
        compiler_params=_params("parallel", "arbitrary"),
        name="conv_ffn",
    )(x, g.reshape(1, D), wu, cw, cb, wd)


def _pad_heads_cols(w, H, d, dp):
    R = w.shape[0]
    return jnp.pad(w.reshape(R, H, d), ((0, 0), (0, 0), (0, dp - d))).reshape(R, H * dp)


def _pad_heads_rows(w, H, d, dp):
    C = w.shape[1]
    return jnp.pad(w.reshape(H, d, C), ((0, 0), (0, dp - d), (0, 0))).reshape(H * dp, C)


def _pad_cols(w, n):
    return jnp.pad(w, ((0, 0), (0, n - w.shape[1])))


def _cols(w, sizes):
    out, c = [], 0
    for s in sizes:
        out.append(w[:, c:c + s])
        c += s
    return out


def _xq_cols(w):
    return _pad_heads_cols(w, XA_HEADS, XA_HEAD_DIM, LANE)


def _out_weights(w_out, d_tok, tok_pad=None):
    wt, wx = w_out[:d_tok], w_out[d_tok:]
    if tok_pad is not None:
        wt = _pad_heads_rows(wt, *tok_pad)
    wx = _pad_heads_rows(wx, XA_HEADS, XA_HEAD_DIM, LANE)
    return wt.astype(BF16), wx.astype(BF16)


def _gla_layer(xf, norm_g, B, S, w_in, w_gate2, b_gate, o_norm):
    D = xf.shape[1]
    d_mix = 3 * D // 4
    H = GLA_HEADS
    dk, dv = d_mix // 2 // H, d_mix // H
    KP, VP = -(-dk // LANE) * LANE, -(-dv // LANE) * LANE
    q, k, v, glr, og, xq = _cols(w_in, [H * dk, H * dk, d_mix, GLA_RANK, d_mix, XA_HEADS * XA_HEAD_DIM])
    w = jnp.concatenate([_pad_heads_cols(q, H, dk, KP), _pad_heads_cols(k, H, dk, KP),
                         _pad_heads_cols(v, H, dv, VP), _pad_heads_cols(og, H, dv, VP),
                         _xq_cols(xq), _pad_cols(glr, LANE)], axis=1).astype(BF16)
    segs = [H * KP, H * KP, H * VP, H * VP, XA_HEADS * LANE, LANE]
    q, k, v, og, xq, glr = _norm_matmul(xf, norm_g, w, segs, [F32, F32, BF16, F32, F32, F32])
    wg = jnp.pad(_pad_heads_cols(w_gate2, H, dk, KP), ((0, LANE - GLA_RANK), (0, 0)))
    bg = _pad_heads_cols(b_gate.reshape(1, -1), H, dk, KP)
    on = _pad_cols(o_norm.reshape(1, -1), VP)
    r3 = lambda a: a.reshape(B, S, -1)
    tok = _gla_mixer(r3(q), r3(k), r3(v), r3(og), r3(glr), wg, bg, on,
                     H=H, KP=KP, VP=VP, dk=dk, dv=dv)
    return tok, xq, (H, dv, VP)


def _dilated_layer(xf, norm_g, B, S, w_in, q_norm, k_norm, positions):
    NG = len(DIL_GROUPS)
    GW = DIL_HEADS * DIL_HEAD_DIM
    q, k, v, xq = _cols(w_in, [NG * GW] * 3 + [XA_HEADS * XA_HEAD_DIM])
    w = jnp.concatenate([q, k, v, _xq_cols(xq)], axis=1).astype(BF16)
    q, k, v, xq = _norm_matmul(xf, norm_g, w, [NG * GW] * 3 + [XA_HEADS * LANE],
                               [F32, F32, BF16, F32])
    qr, kr = _qk_rope(q, k, positions.reshape(B * S, 1), q_norm, k_norm)
    r3 = lambda a: a.reshape(B, S, -1)
    outs, lses = [], []
    for g, (window, dilation) in enumerate(DIL_GROUPS):
        o, lse = _dil_attention(r3(qr), r3(kr), r3(v), g, window, dilation)
        outs.append(o)
        lses.append(lse)
    tok = _dil_merge(outs, lses)
    return tok.reshape(B, S, GW), xq, None


def _mamba_layer(xf, norm_g, B, S, w_in, conv_w, conv_b, dt_bias, a_log, d_skip, norm_gain):
    D = xf.shape[1]
    d_mix = 3 * D // 4
    GN = SSM_GROUPS * SSM_STATE
    NHD = d_mix // SSM_HEAD_DIM
    z, xbc, dt, xq = _cols(w_in, [d_mix, d_mix + 2 * GN, NHD, XA_HEADS * XA_HEAD_DIM])
    w = jnp.concatenate([z, xbc, _pad_cols(dt, LANE), _xq_cols(xq)], axis=1).astype(BF16)
    z, xbc, dt, xq = _norm_matmul(xf, norm_g, w, [d_mix, d_mix + 2 * GN, LANE, XA_HEADS * LANE],
                                  [F32, F32, F32, F32])
    padl = lambda a: _pad_cols(a.reshape(1, -1).astype(F32), LANE)
    head = lax.broadcasted_iota(jnp.int32, (LANE, d_mix), 0)
    col = lax.broadcasted_iota(jnp.int32, (LANE, d_mix), 1)
    e01 = (col // SSM_HEAD_DIM == head).astype(BF16)
    r3 = lambda a: a.reshape(B, S, -1)
    tok = _ssd_mixer(r3(z), r3(xbc), r3(dt), conv_w, conv_b.reshape(1, -1), padl(dt_bias),
                     padl(a_log), jnp.repeat(d_skip, SSM_HEAD_DIM).reshape(1, -1),
                     norm_gain.reshape(1, -1), e01)
    return tok, xq, None


def _hgrn_layer(xf, norm_g, B, S, w_in, lower_bounds, o_norm, layer):
    D = xf.shape[1]
    d_mix = 3 * D // 4
    H = d_mix // HGRN_EXPAND
    KP, VP = HGRN_EXPAND, d_mix // H
    q, f, i, og, xq = _cols(w_in, [H * KP, H * KP, d_mix, d_mix, XA_HEADS * XA_HEAD_DIM])
    w = jnp.concatenate([q, f, i, og, _xq_cols(xq)], axis=1).astype(BF16)
    q, f, i, og, xq = _norm_matmul(xf, norm_g, w, [H * KP, H * KP, d_mix, d_mix, XA_HEADS * LANE],
                                   [F32, F32, BF16, F32, F32])
    lbs = jnp.cumsum(jax.nn.softmax(lower_bounds.astype(F32), axis=0), axis=0)
    lb = (lbs[layer] - lbs[0]).reshape(1, -1)
    r3 = lambda a: a.reshape(B, S, -1)
    tok = _hgrn_mixer(r3(q), r3(f), r3(i), r3(og), lb, o_norm.reshape(1, -1),
                      H=H, KP=KP, VP=VP, dv=VP)
    return tok, xq, None


def kernel(x, mem, positions, mem_norm, mix_norm, xa_w_kv, xa_q_norm, xa_k_norm, ffn_norm, ffn_w_up, ffn_conv_w, ffn_conv_b, ffn_w_down, a_w_in, a_w_gate2, a_b_gate, a_o_norm, a_w_out, b_w_in, b_q_norm, b_k_norm, b_w_out, c_w_in, c_conv_w, c_conv_b, c_dt_bias, c_a_log, c_d, c_norm, c_w_out, d_w_in, d_lower_bounds, d_o_norm, d_w_out):
    B, S, D = x.shape
    depth = mix_norm.shape[0]
    Mm = mem.shape[1]
    d_xa = XA_HEADS * XA_HEAD_DIM

    wkv = []
    for i in range(depth):
        kw, vw = xa_w_kv[i][:, :d_xa], xa_w_kv[i][:, d_xa:]
        wkv += [_xq_cols(kw), _xq_cols(vw)]
    wkv = jnp.concatenate(wkv, axis=1).astype(BF16)
    (kv_all,) = _norm_matmul(mem.reshape(B * Mm, D), mem_norm, wkv, [wkv.shape[1]], [F32])
    kv_all = kv_all.reshape(B, Mm, -1)
    pad_xn = lambda g: _pad_cols(g.reshape(1, -1), LANE)

    xf = x.reshape(B * S, D)
    for i in range(depth):
        kind = i % N_MIXERS
        if kind == 0:
            tok, xq, tok_pad = _gla_layer(xf, mix_norm[i], B, S, a_w_in, a_w_gate2, a_b_gate, a_o_norm)
            w_out, d_tok = a_w_out, 3 * D // 4
        elif kind == 1:
            tok, xq, tok_pad = _dilated_layer(xf, mix_norm[i], B, S, b_w_in, b_q_norm, b_k_norm, positions)
            w_out, d_tok = b_w_out, DIL_HEADS * DIL_HEAD_DIM
        elif kind == 2:
            tok, xq, tok_pad = _mamba_layer(xf, mix_norm[i], B, S, c_w_in, c_conv_w, c_conv_b,
                                            c_dt_bias, c_a_log, c_d, c_norm)
            w_out, d_tok = c_w_out, 3 * D // 4
        else:
            tok, xq, tok_pad = _hgrn_layer(xf, mix_norm[i], B, S, d_w_in, d_lower_bounds, d_o_norm, i)
            w_out, d_tok = d_w_out, 3 * D // 4
        xa = _xattn(xq.reshape(B, S, -1), kv_all, i, pad_xn(xa_q_norm[i]), pad_xn(xa_k_norm[i]))
        wt, wx = _out_weights(w_out, d_tok, tok_pad)
        xf = _out_proj(xf, tok.reshape(B * S, -1), xa.reshape(B * S, -1), wt, wx)
        xf = _ffn(xf.reshape(B, S, D), ffn_norm[i], ffn_w_up[i].astype(BF16), ffn_conv_w[i],
                  ffn_conv_b[i].reshape(1, -1), ffn_w_down[i].astype(BF16)).reshape(B * S, D)
    return xf.reshape(B, S, D)
```

```python
import functools
import math

import jax
import jax.numpy as jnp
from jax import lax
from jax.experimental import pallas as pl
from jax.experimental.pallas import tpu as pltpu

F32 = jnp.float32
BF16 = jnp.bfloat16

LANE = 128
SUBLANE = 8
VMEM_LIMIT = 56 * 1024 * 1024

EPS = 1e-6
ROPE_THETA = 10000.0
CHUNK = 64
N_MIXERS = 4
XA_HEADS, XA_HEAD_DIM = 4, 64
GLA_HEADS, GLA_RANK, GLA_GATE_NORM = 4, 16, 16.0
DIL_GROUPS = ((128, 1), (512, 4), (2048, 16))
DIL_HEADS, DIL_HEAD_DIM, DIL_BLOCK = 4, 128, 128
SSM_HEAD_DIM, SSM_GROUPS, SSM_STATE = 64, 2, 128
HGRN_EXPAND = 128

TOKEN_TILE = 512
SEQ_TILE = 512
COL_CHUNK = 512


def _params(*sem):
    return pltpu.CompilerParams(dimension_semantics=sem, vmem_limit_bytes=VMEM_LIMIT)


def _dot(a, b):
    return jnp.dot(a.astype(BF16), b.astype(BF16), preferred_element_type=F32)


def _dot_nt(a, b):
    return lax.dot_general(a.astype(BF16), b.astype(BF16), (((1,), (1,)), ((), ())),
                           preferred_element_type=F32)


def _dot_tn(a, b):
    return lax.dot_general(a.astype(BF16), b.astype(BF16), (((0,), (0,)), ((), ())),
                           preferred_element_type=F32)


def _split(x):
    hi = x.astype(BF16)
    return hi, (x - hi.astype(F32)).astype(BF16)


def _dot_exact_rhs(a, b01):
    hi, lo = _split(a)
    return (jnp.dot(hi, b01, preferred_element_type=F32)
            + jnp.dot(lo, b01, preferred_element_type=F32))


def _cumsum_rows(tri01, x):
    hi, lo = _split(x)
    return (jnp.dot(tri01, hi, preferred_element_type=F32)
            + jnp.dot(tri01, lo, preferred_element_type=F32))


def _dot_hilo(a, b):
    ah, al = _split(a)
    bh, bl = _split(b)
    return (jnp.dot(ah, bh, preferred_element_type=F32)
            + jnp.dot(ah, bl, preferred_element_type=F32)
            + jnp.dot(al, bh, preferred_element_type=F32))


def _sigmoid(x):
    return 1.0 / (1.0 + jnp.exp(-x))


def _silu(x):
    return x * _sigmoid(x)


def _softplus(x):
    return jnp.maximum(x, 0.0) + jnp.log1p(jnp.exp(-jnp.abs(x)))


def _log_sigmoid(x):
    return -_softplus(-x)


def _tril01(n):
    r = lax.broadcasted_iota(jnp.int32, (n, n), 0)
    c = lax.broadcasted_iota(jnp.int32, (n, n), 1)
    return r >= c


def _norm_matmul_kernel(x_ref, g_ref, w_ref, *o_refs, segs):
    x = x_ref[...]
    h = x * lax.rsqrt(jnp.mean(x * x, axis=-1, keepdims=True) + EPS)
    h = (h * g_ref[...]).astype(BF16)
    c0 = 0
    for o_ref, width in zip(o_refs, segs):
        for j in range(0, width, COL_CHUNK):
            w = min(COL_CHUNK, width - j)
            o_ref[:, j:j + w] = jnp.dot(h, w_ref[:, c0 + j:c0 + j + w],
                                        preferred_element_type=F32).astype(o_ref.dtype)
        c0 += width


def _norm_matmul(x, g, w, segs, dtypes, tm=TOKEN_TILE):
    M, D = x.shape
    tm = min(tm, M)
    assert M % tm == 0 and w.shape == (D, sum(segs)) and all(s % LANE == 0 for s in segs)
    return pl.pallas_call(
        functools.partial(_norm_matmul_kernel, segs=tuple(segs)),
        grid=(M // tm,),
        in_specs=[pl.BlockSpec((tm, D), lambda i: (i, 0)),
                  pl.BlockSpec((1, D), lambda i: (0, 0)),
                  pl.BlockSpec(w.shape, lambda i: (0, 0))],
        out_specs=[pl.BlockSpec((tm, s), lambda i: (i, 0)) for s in segs],
        out_shape=[jax.ShapeDtypeStruct((M, s), dt) for s, dt in zip(segs, dtypes)],
        compiler_params=_params("parallel"),
        name="norm_matmul",
    )(x, g.reshape(1, D).astype(F32), w)


def _gla_chunk_heads(q, k, v, la, tri01, tril, st_ref, H, KP, VP):
    C = q.shape[0]
    b = _cumsum_rows(tri01, la)
    outs = []
    for h in range(H):
        ks = slice(h * KP, (h + 1) * KP)
        bh = b[:, ks]
        bl = b[C - 1:C, ks]
        br = b[C // 2 - 1:C // 2, ks]
        qh, kh, vh = q[:, ks], k[:, ks], v[:, h * VP:(h + 1) * VP]
        att = _dot_nt(qh * jnp.exp(bh - br), kh * jnp.exp(br - bh))
        att = jnp.where(tril, att, 0.0)
        st = st_ref[h]
        o = _dot(att, vh) + _dot_nt(qh * jnp.exp(bh), st)
        st_ref[h] = jnp.exp(bl) * st + _dot_tn(vh, kh * jnp.exp(bl - bh))
        outs.append(o)
    return outs


def _head_rms_gate(o, g, gate, dv):
    y = o * lax.rsqrt(jnp.sum(o * o, axis=-1, keepdims=True) * (1.0 / dv) + EPS)
    return (y * g) * gate


def _gla_kernel(q_ref, k_ref, v_ref, og_ref, glr_ref, wg_ref, bg_ref, on_ref, o_ref, st_ref,
                *, H, KP, VP, dk, dv, T):
    @pl.when(pl.program_id(1) == 0)
    def _():
        st_ref[...] = jnp.zeros_like(st_ref)

    tril = _tril01(CHUNK)
    tri01 = tril.astype(BF16)
    wg = wg_ref[...]
    bg = bg_ref[...]
    on = on_ref[...]

    def chunk(c, carry):
        rows = pl.ds(pl.multiple_of(c * CHUNK, CHUNK), CHUNK)
        la = _log_sigmoid(_dot_hilo(glr_ref[rows, :], wg) + bg) * (1.0 / GLA_GATE_NORM)
        q = q_ref[rows, :] * (dk ** -0.5)
        outs = _gla_chunk_heads(q, k_ref[rows, :], v_ref[rows, :], la, tri01, tril, st_ref,
                                H, KP, VP)
        og = og_ref[rows, :]
        for h, o in enumerate(outs):
            vs = slice(h * VP, (h + 1) * VP)
            o_ref[rows, vs] = _head_rms_gate(o, on, _silu(og[:, vs]), dv).astype(o_ref.dtype)
        return carry

    lax.fori_loop(0, T // CHUNK, chunk, 0, unroll=2)


def _hgrn_kernel(q_ref, f_ref, v_ref, og_ref, lb_ref, on_ref, o_ref, st_ref, *, H, KP, VP, dv, T):
    @pl.when(pl.program_id(1) == 0)
    def _():
        st_ref[...] = jnp.zeros_like(st_ref)

    tril = _tril01(CHUNK)
    tri01 = tril.astype(BF16)
    lb = lb_ref[...]
    on = on_ref[...]

    def chunk(c, carry):
        rows = pl.ds(pl.multiple_of(c * CHUNK, CHUNK), CHUNK)
        fg = lb + (1.0 - lb) * _sigmoid(f_ref[rows, :])
        q = _silu(q_ref[rows, :])
        outs = _gla_chunk_heads(q, 1.0 - fg, v_ref[rows, :], jnp.log(fg), tri01, tril, st_ref,
                                H, KP, VP)
        og = og_ref[rows, :]
        for h, o in enumerate(outs):
            vs = slice(h * VP, (h + 1) * VP)
            o_ref[rows, vs] = _head_rms_gate(o, on, _sigmoid(og[:, vs]), dv).astype(o_ref.dtype)
        return carry

    lax.fori_loop(0, T // CHUNK, chunk, 0, unroll=2)


def _seq_spec(T, width, col=0):
    return pl.BlockSpec((None, T, width), lambda b, t: (b, t, col))


def _const_spec(shape):
    return pl.BlockSpec(shape, lambda b, t: (0,) * len(shape))


def _gla_mixer(q, k, v, og, glr, wg, bg, on, *, H, KP, VP, dk, dv):
    B, S, _ = q.shape
    T = min(SEQ_TILE, S)
    assert S % T == 0 and T % CHUNK == 0
    return pl.pallas_call(
        functools.partial(_gla_kernel, H=H, KP=KP, VP=VP, dk=dk, dv=dv, T=T),
        grid=(B, S // T),
        in_specs=[_seq_spec(T, H * KP), _seq_spec(T, H * KP), _seq_spec(T, H * VP),
                  _seq_spec(T, H * VP), _seq_spec(T, LANE),
                  _const_spec(wg.shape), _const_spec(bg.shape), _const_spec(on.shape)],
        out_specs=_seq_spec(T, H * VP),
        out_shape=jax.ShapeDtypeStruct((B, S, H * VP), BF16),
        scratch_shapes=[pltpu.VMEM((H, VP, KP), F32)],
        compiler_params=_params("parallel", "arbitrary"),
        name="gla_mixer",
    )(q, k, v, og, glr, wg, bg, on)


def _hgrn_mixer(q, f, v, og, lb, on, *, H, KP, VP, dv):
    B, S, _ = q.shape
    T = min(SEQ_TILE, S)
    assert S % T == 0 and T % CHUNK == 0
    return pl.pallas_call(
        functools.partial(_hgrn_kernel, H=H, KP=KP, VP=VP, dv=dv, T=T),
        grid=(B, S // T),
        in_specs=[_seq_spec(T, H * KP), _seq_spec(T, H * KP), _seq_spec(T, H * VP),
                  _seq_spec(T, H * VP), _const_spec(lb.shape), _const_spec(on.shape)],
        out_specs=_seq_spec(T, H * VP),
        out_shape=jax.ShapeDtypeStruct((B, S, H * VP), BF16),
        scratch_shapes=[pltpu.VMEM((H, VP, KP), F32)],
        compiler_params=_params("parallel", "arbitrary"),
        name="hgrn_mixer",
    )(q, f, v, og, lb, on)


def _ssd_kernel(z_ref, xbc_ref, dt_ref, cw_ref, cb_ref, dtb_ref, alog_ref, dsk_ref, ng_ref, e_ref,
                o_ref, carry_ref, xc_ref, st_ref, *, T, DM, N, G):
    @pl.when(pl.program_id(1) == 0)
    def _():
        st_ref[...] = jnp.zeros_like(st_ref)
        carry_ref[...] = jnp.zeros_like(carry_ref)

    C = CHUNK
    GW = DM // G
    tril = _tril01(C)
    tri01 = tril.astype(BF16)
    e01 = e_ref[...]
    a_neg = -jnp.exp(alog_ref[...])
    dtb = dtb_ref[...]
    cbias = cb_ref[...]
    dsk = dsk_ref[...]
    ng = ng_ref[...]
    P = SSM_HEAD_DIM
    rr = lax.broadcasted_iota(jnp.int32, (C, DM), 0)
    cc = lax.broadcasted_iota(jnp.int32, (C, DM), 1)
    diag_sel = (cc & (P - 1)) == rr
    r2 = lax.broadcasted_iota(jnp.int32, (C, 2 * P), 0)
    c2 = lax.broadcasted_iota(jnp.int32, (C, 2 * P), 1)
    causal2 = r2 >= (c2 & (P - 1))
    lo_half = lax.broadcasted_iota(jnp.int32, (C, 2 * P), 1) < P

    cw = cw_ref[...]
    for r0 in range(0, T, C):
        prev = carry_ref[...] if r0 == 0 else xbc_ref[r0 - SUBLANE:r0, :]
        xc_ref[r0:r0 + C, :] = _silu(_causal_taps(xbc_ref[r0:r0 + C, :], prev, cw, cbias))
    carry_ref[...] = xbc_ref[T - SUBLANE:T, :]

    def chunk(c, carry):
        rows = pl.ds(pl.multiple_of(c * C, C), C)
        xc = xc_ref[rows, :]
        xs = xc[:, :DM]
        dt = _softplus(dt_ref[rows, :] + dtb)
        acs = _cumsum_rows(tri01, dt * a_neg)
        dt_e = _dot_exact_rhs(dt, e01)
        acs_e = _dot_exact_rhs(acs, e01)
        acs_row = jnp.sum(jnp.where(diag_sel, acs_e, 0.0), axis=0, keepdims=True)
        acs_last = acs_e[C - 1:C, :]
        xdt = xs * dt_e
        z = z_ref[rows, :]
        for g in range(G):
            gs = slice(g * GW, (g + 1) * GW)
            bm = xc[:, DM + g * N:DM + (g + 1) * N]
            cm = xc[:, DM + G * N + g * N:DM + G * N + (g + 1) * N]
            cb2 = _dot_nt(cm, jnp.concatenate([bm, bm], axis=0))
            yd = []
            for p in range(GW // (2 * P)):
                ps = slice(g * GW + p * 2 * P, g * GW + (p + 1) * 2 * P)
                seg = acs_e[:, ps] - acs_row[:, ps]
                lmat = jnp.exp(jnp.where(causal2, seg, -jnp.inf))
                xp = xdt[:, ps]
                rhs = jnp.concatenate([jnp.where(lo_half, xp, 0.0), jnp.where(lo_half, 0.0, xp)],
                                      axis=0)
                yd.append(_dot(cb2 * lmat, rhs))
            y = jnp.concatenate(yd, axis=1)
            st = st_ref[g]
            y = y + _dot(cm, st) * jnp.exp(acs_e[:, gs])
            xe = xdt[:, gs] * jnp.exp(acs_last[:, gs] - acs_e[:, gs])
            st_ref[g] = jnp.exp(acs_last[:, gs]) * st + _dot_tn(bm, xe)
            y = y + dsk[:, gs] * xs[:, gs]
            y = y * _silu(z[:, gs])
            y = y * lax.rsqrt(jnp.mean(y * y, axis=-1, keepdims=True) + EPS)
            o_ref[rows, gs] = (y * ng[:, gs]).astype(o_ref.dtype)
        return carry

    lax.fori_loop(0, T // C, chunk, 0, unroll=2)


def _ssd_mixer(z, xbc, dt, cw, cb, dtb, alog, dsk, ng, e01):
    B, S, DM = z.shape
    W = xbc.shape[-1]
    T = min(SEQ_TILE, S)
    assert S % T == 0 and T % CHUNK == 0
    G, N = SSM_GROUPS, SSM_STATE
    return pl.pallas_call(
        functools.partial(_ssd_kernel, T=T, DM=DM, N=N, G=G),
        grid=(B, S // T),
        in_specs=[_seq_spec(T, DM), _seq_spec(T, W), _seq_spec(T, LANE),
                  _const_spec(cw.shape), _const_spec(cb.shape), _const_spec(dtb.shape),
                  _const_spec(alog.shape), _const_spec(dsk.shape), _const_spec(ng.shape),
                  _const_spec(e01.shape)],
        out_specs=_seq_spec(T, DM),
        out_shape=jax.ShapeDtypeStruct((B, S, DM), BF16),
        scratch_shapes=[pltpu.VMEM((SUBLANE, W), F32), pltpu.VMEM((T, W), F32),
                        pltpu.VMEM((G, N, DM // G), F32)],
        compiler_params=_params("parallel", "arbitrary"),
        name="ssd_mixer",
    )(z, xbc, dt, cw, cb, dtb, alog, dsk, ng, e01)


def _dil_in_proj_kernel(x_ref, g_ref, w_ref, pos_ref, qn_ref, kn_ref, invf_ref, sign_ref, *refs,
                        dils, NH, HD):
    NG = len(dils)
    GW = NH * HD
    outs, xq_ref, buf_ref = refs[:3 * NG], refs[3 * NG], refs[3 * NG + 1]
    tm = x_ref.shape[0]
    x = x_ref[...]
    h = x * lax.rsqrt(jnp.mean(x * x, axis=-1, keepdims=True) + EPS)
    h = (h * g_ref[...]).astype(BF16)
    ang = pos_ref[...].astype(F32) * invf_ref[...]
    cos = jnp.cos(ang)
    sin = jnp.sin(ang) * sign_ref[...]
    gains = (qn_ref[...], kn_ref[...])
    for kind in range(3):
        for g, r in enumerate(dils):
            c0 = (kind * NG + g) * GW
            u = jnp.dot(h, w_ref[:, c0:c0 + GW], preferred_element_type=F32)
            o_ref = outs[kind * NG + g]
            for hh in range(NH):
                hs = slice(hh * HD, (hh + 1) * HD)
                y = u[:, hs]
                if kind < 2:
                    y = (y * lax.rsqrt(jnp.mean(y * y, axis=-1, keepdims=True) + EPS)) * gains[kind]
                    y = y * cos + pltpu.roll(y, HD // 2, 1) * sin
                if r == 1:
                    o_ref[0, :, hs] = y.astype(o_ref.dtype)
                else:
                    slab = ((kind * NG + g) % 2) * NH + hh
                    buf_ref[slab] = y
                    for rho in range(r):
                        o_ref[rho, :, hs] = buf_ref[slab, pl.ds(rho, tm // r, stride=r), :].astype(
                            o_ref.dtype)
    xq_ref[...] = jnp.dot(h, w_ref[:, 3 * NG * GW:], preferred_element_type=F32)


def _dil_in_proj(x, g, w, pos, qn, kn, dils):
    B, S, D = x.shape
    NH, HD = DIL_HEADS, DIL_HEAD_DIM
    GW = NH * HD
    NG = len(dils)
    T = min(TOKEN_TILE, S)
    WQ = w.shape[1] - 3 * NG * GW
    assert S % T == 0 and all(T % (r * 2 * SUBLANE) == 0 for r in dils)
    half = HD // 2
    inv_freq = ROPE_THETA ** (-jnp.arange(half, dtype=F32) / half)
    invf = jnp.concatenate([inv_freq, inv_freq]).reshape(1, HD)
    sign = jnp.concatenate([-jnp.ones((half,), F32), jnp.ones((half,), F32)]).reshape(1, HD)
    grp_spec = [pl.BlockSpec((None, r, T // r, GW), lambda b, t: (b, 0, t, 0)) for r in dils]
    grp_shape = [jax.ShapeDtypeStruct((B, r, S // r, GW), BF16) for r in dils]
    res = pl.pallas_call(
        functools.partial(_dil_in_proj_kernel, dils=tuple(dils), NH=NH, HD=HD),
        grid=(B, S // T),
        in_specs=[_seq_spec(T, D), _const_spec((1, D)),
                  pl.BlockSpec(w.shape, lambda b, t: (0, 0), pipeline_mode=pl.Buffered(1)),
                  _seq_spec(T, 1), _const_spec((1, HD)), _const_spec((1, HD)),
                  _const_spec((1, HD)), _const_spec((1, HD))],
        out_specs=grp_spec * 3 + [_seq_spec(T, WQ)],
        out_shape=grp_shape * 3 + [jax.ShapeDtypeStruct((B, S, WQ), F32)],
        scratch_shapes=[pltpu.VMEM((2 * NH, T, HD), F32)],
        compiler_params=_params("parallel", "parallel"),
        name="dilated_in_proj",
    )(x, g.reshape(1, D).astype(F32), w, pos, qn.reshape(1, HD), kn.reshape(1, HD), invf, sign)
    return res[:NG], res[NG:2 * NG], res[2 * NG:3 * NG], res[3 * NG]


def _dil_attn_kernel(q_ref, kp_ref, kc_ref, vp_ref, vc_ref, o_ref, lse_ref, *, nq, W, NH, HD):
    n0 = pl.program_id(2)
    Q = DIL_BLOCK
    i_idx = lax.broadcasted_iota(jnp.int32, (Q, 2 * Q), 0)
    j_idx = lax.broadcasted_iota(jnp.int32, (Q, 2 * Q), 1)
    dist = Q + i_idx - j_idx
    band = (dist >= 0) & (dist <= W)
    jmin = jnp.where(n0 > 0, 0, Q)
    band_first = band & (j_idx >= jmin)
    lane = lax.broadcasted_iota(jnp.int32, (Q, LANE), 1)
    scale = HD ** -0.5
    for i in range(nq):
        rows = slice(i * Q, (i + 1) * Q)
        mask = band_first if i == 0 else band
        lse_tile = jnp.zeros((Q, LANE), F32)
        for h in range(NH):
            hs = slice(h * HD, (h + 1) * HD)
            if i == 0:
                kk = jnp.concatenate([kp_ref[:, hs], kc_ref[0:Q, hs]], axis=0)
                vv = jnp.concatenate([vp_ref[:, hs], vc_ref[0:Q, hs]], axis=0)
            else:
                kk = kc_ref[(i - 1) * Q:(i + 1) * Q, hs]
                vv = vc_ref[(i - 1) * Q:(i + 1) * Q, hs]
            s = _dot_nt(q_ref[rows, hs], kk) * scale
            s = jnp.where(mask, s, -jnp.inf)
            m = jnp.max(s, axis=-1, keepdims=True)
            p = jnp.exp(s - m)
            l = jnp.sum(p, axis=-1, keepdims=True)
            o_ref[rows, hs] = _dot(p, vv) * (1.0 / l)
            lse_tile = jnp.where(lane == h, m + jnp.log(l), lse_tile)
        lse_ref[rows, :] = lse_tile


def _dil_attention(q, k, v, window, dilation):
    B, r, L, GW = q.shape
    NH, HD, Q = DIL_HEADS, DIL_HEAD_DIM, DIL_BLOCK
    W = window // dilation
    assert r == dilation and L % Q == 0 and W <= Q
    nq = min(4, L // Q)
    assert (L // Q) % nq == 0
    cur = pl.BlockSpec((None, None, nq * Q, GW), lambda b, rho, n: (b, rho, n, 0))
    prev = pl.BlockSpec((None, None, Q, GW),
                        lambda b, rho, n: (b, rho, jnp.maximum(n * nq - 1, 0), 0))
    return pl.pallas_call(
        functools.partial(_dil_attn_kernel, nq=nq, W=W, NH=NH, HD=HD),
        grid=(B, r, L // (nq * Q)),
        in_specs=[cur, prev, cur, prev, cur],
        out_specs=[cur, pl.BlockSpec((None, None, nq * Q, LANE), lambda b, rho, n: (b, rho, n, 0))],
        out_shape=[jax.ShapeDtypeStruct((B, r, L, GW), F32),
                   jax.ShapeDtypeStruct((B, r, L, LANE), F32)],
        compiler_params=_params("parallel", "parallel", "arbitrary"),
        name=f"dilated_attention_r{r}",
    )(q, k, k, v, v)


def _dil_merge_kernel(*refs, dils, NH, HD):
    NG = len(dils)
    o_refs, l_refs, out_ref, obuf_ref, lbuf_ref = (refs[:NG], refs[NG:2 * NG], refs[2 * NG],
                                                   refs[2 * NG + 1], refs[2 * NG + 2])
    tm = out_ref.shape[0]
    ls = []
    for g, r in enumerate(dils):
        if r == 1:
            ls.append(l_refs[g][0])
            continue
        for rho in range(r):
            rows = pl.ds(rho, tm // r, stride=r)
            lbuf_ref[g, rows, :] = l_refs[g][rho]
            for h in range(NH):
                obuf_ref[g * NH + h, rows, :] = o_refs[g][rho, :, h * HD:(h + 1) * HD]
        ls.append(lbuf_ref[g])
    m = functools.reduce(jnp.maximum, ls)
    es = [jnp.exp(l - m) for l in ls]
    inv = 1.0 / functools.reduce(lambda a, b: a + b, es)
    for h in range(NH):
        hs = slice(h * HD, (h + 1) * HD)
        acc = None
        for g, r in enumerate(dils):
            o = o_refs[g][0, :, hs] if r == 1 else obuf_ref[g * NH + h]
            term = (es[g][:, h:h + 1] * inv[:, h:h + 1]) * o
            acc = term if acc is None else acc + term
        out_ref[:, hs] = acc.astype(out_ref.dtype)


def _dil_merge(outs, lses, dils):
    B, _, L0, GW = outs[0].shape
    S = L0 * dils[0]
    NH, HD = DIL_HEADS, DIL_HEAD_DIM
    NG = len(dils)
    T = min(TOKEN_TILE, S)
    grp = lambda w: [pl.BlockSpec((None, r, T // r, w), lambda b, t: (b, 0, t, 0)) for r in dils]
    return pl.pallas_call(
        functools.partial(_dil_merge_kernel, dils=tuple(dils), NH=NH, HD=HD),
        grid=(B, S // T),
        in_specs=grp(GW) + grp(LANE),
        out_specs=_seq_spec(T, GW),
        out_shape=jax.ShapeDtypeStruct((B, S, GW), BF16),
        scratch_shapes=[pltpu.VMEM((NG * NH, T, HD), F32), pltpu.VMEM((NG, T, LANE), F32)],
        compiler_params=_params("parallel", "parallel"),
        name="dilated_merge",
    )(*outs, *lses)


def _xattn_kernel(xq_ref, kv_ref, qn_ref, kn_ref, o_ref, *, NH, HP, hd):
    qn = qn_ref[...]
    kn = kn_ref[...]
    scale = hd ** -0.5
    for h in range(NH):
        hs = slice(h * HP, (h + 1) * HP)
        q = xq_ref[:, hs]
        q = (q * lax.rsqrt(jnp.sum(q * q, axis=-1, keepdims=True) * (1.0 / hd) + EPS)) * qn
        k = kv_ref[:, hs]
        k = (k * lax.rsqrt(jnp.sum(k * k, axis=-1, keepdims=True) * (1.0 / hd) + EPS)) * kn
        v = kv_ref[:, NH * HP + h * HP:NH * HP + (h + 1) * HP]
        s = _dot_nt(q, k) * scale
        m = jnp.max(s, axis=-1, keepdims=True)
        p = jnp.exp(s - m)
        l = jnp.sum(p, axis=-1, keepdims=True)
        o_ref[:, hs] = (_dot(p, v) * (1.0 / l)).astype(o_ref.dtype)


def _xattn(xq, kv_all, layer, qn, kn):
    B, S, WQ = xq.shape
    Mm = kv_all.shape[1]
    T = min(SEQ_TILE, S)
    NH, HP = XA_HEADS, LANE
    return pl.pallas_call(
        functools.partial(_xattn_kernel, NH=NH, HP=HP, hd=XA_HEAD_DIM),
        grid=(B, S // T),
        in_specs=[_seq_spec(T, WQ),
                  pl.BlockSpec((None, Mm, 2 * WQ), lambda b, t: (b, 0, layer)),
                  _const_spec(qn.shape), _const_spec(kn.shape)],
        out_specs=_seq_spec(T, WQ),
        out_shape=jax.ShapeDtypeStruct((B, S, WQ), BF16),
        compiler_params=_params("parallel", "parallel"),
        name="memory_xattn",
    )(xq, kv_all, qn, kn)


def _out_proj_kernel(x_ref, tok_ref, xa_ref, wt_ref, wx_ref, o_ref):
    tok = tok_ref[...]
    xa = xa_ref[...]
    D = o_ref.shape[1]
    for j in range(0, D, COL_CHUNK):
        cs = slice(j, j + COL_CHUNK)
        o_ref[:, cs] = (x_ref[:, cs] + jnp.dot(tok, wt_ref[:, cs], preferred_element_type=F32)
                        + jnp.dot(xa, wx_ref[:, cs], preferred_element_type=F32))


def _out_proj(x, tok, xa, wt, wx, tm=TOKEN_TILE):
    M, D = x.shape
    tm = min(tm, M)
    row = lambda w: pl.BlockSpec((tm, w), lambda i: (i, 0))
    full = lambda a: pl.BlockSpec(a.shape, lambda i: (0, 0))
    return pl.pallas_call(
        _out_proj_kernel,
        grid=(M // tm,),
        in_specs=[row(D), row(tok.shape[1]), row(xa.shape[1]), full(wt), full(wx)],
        out_specs=row(D),
        out_shape=jax.ShapeDtypeStruct((M, D), F32),
        compiler_params=_params("parallel"),
        name="out_proj",
    )(x, tok, xa, wt, wx)


def _causal_taps(u, prev, cw, cb):
    K = cw.shape[0]
    sub = lax.broadcasted_iota(jnp.int32, (SUBLANE, u.shape[1]), 0)
    acc = cb + cw[K - 1:K, :] * u
    for k in range(1, K):
        rolled = pltpu.roll(u, k, 0)
        head = jnp.where(sub < k, pltpu.roll(prev, k, 0), rolled[0:SUBLANE, :])
        shifted = jnp.concatenate([head, rolled[SUBLANE:, :]], axis=0)
        acc = acc + cw[K - 1 - k:K - k, :] * shifted
    return acc


def _ffn_kernel(x_ref, g_ref, wu_ref, cw_ref, cb_ref, wd_ref, o_ref, carry_ref, act_ref,
                *, T, FF, FC):
    @pl.when(pl.program_id(1) == 0)
    def _():
        carry_ref[...] = jnp.zeros_like(carry_ref)

    x = x_ref[...]
    h = x * lax.rsqrt(jnp.mean(x * x, axis=-1, keepdims=True) + EPS)
    h = (h * g_ref[...]).astype(BF16)

    def conv(cols):
        u = jnp.dot(h, wu_ref[:, cols], preferred_element_type=F32)
        prev = carry_ref[:, cols]
        carry_ref[:, cols] = u[T - SUBLANE:T, :]
        return _causal_taps(u, prev, cw_ref[:, cols], cb_ref[:, cols])

    for j in range(0, FF, FC):
        gate = conv(slice(j, j + FC))
        val = conv(slice(FF + j, FF + j + FC))
        act_ref[:, j:j + FC] = (_silu(gate) * val).astype(BF16)
    D = o_ref.shape[1]
    for n in range(0, D, COL_CHUNK):
        cs = slice(n, n + COL_CHUNK)
        o_ref[:, cs] = x_ref[:, cs] + jnp.dot(act_ref[...], wd_ref[:, cs],
                                              preferred_element_type=F32)


def _ffn(x, g, wu, cw, cb, wd):
    B, S, D = x.shape
    FF = wd.shape[0]
    T = min(SEQ_TILE, S)
    FC = 256
    assert FF % FC == 0 and S % T == 0 and D % COL_CHUNK == 0
    single = dict(pipeline_mode=pl.Buffered(1))
    return pl.pallas_call(
        functools.partial(_ffn_kernel, T=T, FF=FF, FC=FC),
        grid=(B, S // T),
        in_specs=[_seq_spec(T, D), _const_spec((1, D)),
                  pl.BlockSpec(wu.shape, lambda b, t: (0, 0), **single),
                  _const_spec(cw.shape), _const_spec(cb.shape),
                  pl.BlockSpec(wd.shape, lambda b, t: (0, 0), **single)],
        out_specs=_seq_spec(T, D),
        out_shape=jax.ShapeDtypeStruct((B, S, D), F32),
        scratch_shapes=[pltpu.VMEM((SUBLANE, 2 * FF), F32), pltpu.VMEM((T, FF), BF16)],
        compiler_params=_params("parallel", "arbitrary"),
        name="conv_ffn",
    )(x, g.reshape(1, D), wu, cw, cb, wd)


def _pad_heads_cols(w, H, d, dp):
    R = w.shape[0]
    return jnp.pad(w.reshape(R, H, d), ((0, 0), (0, 0), (0, dp - d))).reshape(R, H * dp)


def _pad_heads_rows(w, H, d, dp):
    C = w.shape[1]
    return jnp.pad(w.reshape(H, d, C), ((0, 0), (0, dp - d), (0, 0))).reshape(H * dp, C)


def _pad_cols(w, n):
    return jnp.pad(w, ((0, 0), (0, n - w.shape[1])))


def _cols(w, sizes):
    out, c = [], 0
    for s in sizes:
        out.append(w[:, c:c + s])
        c += s
    return out


def _xq_cols(w):
    return _pad_heads_cols(w, XA_HEADS, XA_HEAD_DIM, LANE)


def _out_weights(w_out, d_tok, tok_pad=None):
    wt, wx = w_out[:d_tok], w_out[d_tok:]
    if tok_pad is not None:
        wt = _pad_heads_rows(wt, *tok_pad)
    wx = _pad_heads_rows(wx, XA_HEADS, XA_HEAD_DIM, LANE)
    return wt.astype(BF16), wx.astype(BF16)


def _gla_layer(xf, norm_g, B, S, w_in, w_gate2, b_gate, o_norm):
    D = xf.shape[1]
    d_mix = 3 * D // 4
    H = GLA_HEADS
    dk, dv = d_mix // 2 // H, d_mix // H
    KP, VP = -(-dk // LANE) * LANE, -(-dv // LANE) * LANE
    q, k, v, glr, og, xq = _cols(w_in, [H * dk, H * dk, d_mix, GLA_RANK, d_mix, XA_HEADS * XA_HEAD_DIM])
    w = jnp.concatenate([_pad_heads_cols(q, H, dk, KP), _pad_heads_cols(k, H, dk, KP),
                         _pad_heads_cols(v, H, dv, VP), _pad_heads_cols(og, H, dv, VP),
                         _xq_cols(xq), _pad_cols(glr, LANE)], axis=1).astype(BF16)
    segs = [H * KP, H * KP, H * VP, H * VP, XA_HEADS * LANE, LANE]
    q, k, v, og, xq, glr = _norm_matmul(xf, norm_g, w, segs, [F32, F32, BF16, F32, F32, F32])
    wg = jnp.pad(_pad_heads_cols(w_gate2, H, dk, KP), ((0, LANE - GLA_RANK), (0, 0)))
    bg = _pad_heads_cols(b_gate.reshape(1, -1), H, dk, KP)
    on = _pad_cols(o_norm.reshape(1, -1), VP)
    r3 = lambda a: a.reshape(B, S, -1)
    tok = _gla_mixer(r3(q), r3(k), r3(v), r3(og), r3(glr), wg, bg, on,
                     H=H, KP=KP, VP=VP, dk=dk, dv=dv)
    return tok, xq, (H, dv, VP)


def _dilated_layer(xf, norm_g, B, S, w_in, q_norm, k_norm, positions):
    NG = len(DIL_GROUPS)
    GW = DIL_HEADS * DIL_HEAD_DIM
    dils = [d for _, d in DIL_GROUPS]
    q, k, v, xq = _cols(w_in, [NG * GW] * 3 + [XA_HEADS * XA_HEAD_DIM])
    w = jnp.concatenate([q, k, v, _xq_cols(xq)], axis=1).astype(BF16)
    qs, ks, vs, xq = _dil_in_proj(xf.reshape(B, S, -1), norm_g, w, positions.reshape(B, S, 1),
                                  q_norm, k_norm, dils)
    outs, lses = [], []
    for g, (window, dilation) in enumerate(DIL_GROUPS):
        o, lse = _dil_attention(qs[g], ks[g], vs[g], window, dilation)
        outs.append(o)
        lses.append(lse)
    tok = _dil_merge(outs, lses, dils)
    return tok, xq.reshape(B * S, -1), None


def _mamba_layer(xf, norm_g, B, S, w_in, conv_w, conv_b, dt_bias, a_log, d_skip, norm_gain):
    D = xf.shape[1]
    d_mix = 3 * D // 4
    GN = SSM_GROUPS * SSM_STATE
    NHD = d_mix // SSM_HEAD_DIM
    z, xbc, dt, xq = _cols(w_in, [d_mix, d_mix + 2 * GN, NHD, XA_HEADS * XA_HEAD_DIM])
    w = jnp.concatenate([z, xbc, _pad_cols(dt, LANE), _xq_cols(xq)], axis=1).astype(BF16)
    z, xbc, dt, xq = _norm_matmul(xf, norm_g, w, [d_mix, d_mix + 2 * GN, LANE, XA_HEADS * LANE],
                                  [F32, F32, F32, F32])
    padl = lambda a: _pad_cols(a.reshape(1, -1).astype(F32), LANE)
    head = lax.broadcasted_iota(jnp.int32, (LANE, d_mix), 0)
    col = lax.broadcasted_iota(jnp.int32, (LANE, d_mix), 1)
    e01 = (col // SSM_HEAD_DIM == head).astype(BF16)
    r3 = lambda a: a.reshape(B, S, -1)
    tok = _ssd_mixer(r3(z), r3(xbc), r3(dt), conv_w, conv_b.reshape(1, -1), padl(dt_bias),
                     padl(a_log), jnp.repeat(d_skip, SSM_HEAD_DIM).reshape(1, -1),
                     norm_gain.reshape(1, -1), e01)
    return tok, xq, None


def _hgrn_layer(xf, norm_g, B, S, w_in, lower_bounds, o_norm, layer):
    D = xf.shape[1]
    d_mix = 3 * D // 4
    H = d_mix // HGRN_EXPAND
    KP, VP = HGRN_EXPAND, d_mix // H
    q, f, i, og, xq = _cols(w_in, [H * KP, H * KP, d_mix, d_mix, XA_HEADS * XA_HEAD_DIM])
    w = jnp.concatenate([q, f, i, og, _xq_cols(xq)], axis=1).astype(BF16)
    q, f, i, og, xq = _norm_matmul(xf, norm_g, w, [H * KP, H * KP, d_mix, d_mix, XA_HEADS * LANE],
                                   [F32, F32, BF16, F32, F32])
    lbs = jnp.cumsum(jax.nn.softmax(lower_bounds.astype(F32), axis=0), axis=0)
    lb = (lbs[layer] - lbs[0]).reshape(1, -1)
    r3 = lambda a: a.reshape(B, S, -1)
    tok = _hgrn_mixer(r3(q), r3(f), r3(i), r3(og), lb, o_norm.reshape(1, -1),
                      H=H, KP=KP, VP=VP, dv=VP)
    return tok, xq, None


def kernel(x, mem, positions, mem_norm, mix_norm, xa_w_kv, xa_q_norm, xa_k_norm, ffn_norm, ffn_w_up, ffn_conv_w, ffn_conv_b, ffn_w_down, a_w_in, a_w_gate2, a_b_gate, a_o_norm, a_w_out, b_w_in, b_q_norm, b_k_norm, b_w_out, c_w_in, c_conv_w, c_conv_b, c_dt_bias, c_a_log, c_d, c_norm, c_w_out, d_w_in, d_lower_bounds, d_o_norm, d_w_out):
    B, S, D = x.shape
    depth = mix_norm.shape[0]
    Mm = mem.shape[1]
    d_xa = XA_HEADS * XA_HEAD_DIM

    wkv = []
    for i in range(depth):
        kw, vw = xa_w_kv[i][:, :d_xa], xa_w_kv[i][:, d_xa:]
        wkv += [_xq_cols(kw), _xq_cols(vw)]
    wkv = jnp.concatenate(wkv, axis=1).astype(BF16)
    (kv_all,) = _norm_matmul(mem.reshape(B * Mm, D), mem_norm, wkv, [wkv.shape[1]], [F32])
    kv_all = kv_all.reshape(B, Mm, -1)
    pad_xn = lambda g: _pad_cols(g.reshape(1, -1), LANE)

    xf = x.reshape(B * S, D)
    for i in range(depth):
        kind = i % N_MIXERS
        if kind == 0:
            tok, xq, tok_pad = _gla_layer(xf, mix_norm[i], B, S, a_w_in, a_w_gate2, a_b_gate, a_o_norm)
            w_out, d_tok = a_w_out, 3 * D // 4
        elif kind == 1:
            tok, xq, tok_pad = _dilated_layer(xf, mix_norm[i], B, S, b_w_in, b_q_norm, b_k_norm, positions)
            w_out, d_tok = b_w_out, DIL_HEADS * DIL_HEAD_DIM
        elif kind == 2:
            tok, xq, tok_pad = _mamba_layer(xf, mix_norm[i], B, S, c_w_in, c_conv_w, c_conv_b,
                                            c_dt_bias, c_a_log, c_d, c_norm)
            w_out, d_tok = c_w_out, 3 * D // 4
        else:
            tok, xq, tok_pad = _hgrn_layer(xf, mix_norm[i], B, S, d_w_in, d_lower_bounds, d_o_norm, i)
            w_out, d_tok = d_w_out, 3 * D // 4
        xa = _xattn(xq.reshape(B, S, -1), kv_all, i, pad_xn(xa_q_norm[i]), pad_xn(xa_k_norm[i]))
        wt, wx = _out_weights(w_out, d_tok, tok_pad)
        xf = _out_proj(xf, tok.reshape(B * S, -1), xa.reshape(B * S, -1), wt, wx)
        xf = _ffn(xf.reshape(B, S, D), ffn_norm[i], ffn_w_up[i].astype(BF16), ffn_conv_w[i],
                  ffn_conv_b[i].reshape(1, -1), ffn_w_down[i].astype(BF16)).reshape(B * S, D)
    return xf.reshape(B, S, D)
```

```python
import functools
import math

import jax
import jax.numpy as jnp
from jax import lax
from jax.experimental import pallas as pl
from jax.experimental.pallas import tpu as pltpu

F32 = jnp.float32
BF16 = jnp.bfloat16

LANE = 128
SUBLANE = 8
VMEM_LIMIT = 56 * 1024 * 1024

EPS = 1e-6
ROPE_THETA = 10000.0
CHUNK = 64
N_MIXERS = 4
XA_HEADS, XA_HEAD_DIM = 4, 64
GLA_HEADS, GLA_RANK, GLA_GATE_NORM = 4, 16, 16.0
DIL_GROUPS = ((128, 1), (512, 4), (2048, 16))
DIL_HEADS, DIL_HEAD_DIM, DIL_BLOCK = 4, 128, 128
SSM_HEAD_DIM, SSM_GROUPS, SSM_STATE = 64, 2, 128
HGRN_EXPAND = 128

TOKEN_TILE = 512
SEQ_TILE = 512
COL_CHUNK = 512


def _params(*sem):
    return pltpu.CompilerParams(dimension_semantics=sem, vmem_limit_bytes=VMEM_LIMIT)


def _dot(a, b):
    return jnp.dot(a.astype(BF16), b.astype(BF16), preferred_element_type=F32)


def _dot_nt(a, b):
    return lax.dot_general(a.astype(BF16), b.astype(BF16), (((1,), (1,)), ((), ())),
                           preferred_element_type=F32)


def _dot_tn(a, b):
    return lax.dot_general(a.astype(BF16), b.astype(BF16), (((0,), (0,)), ((), ())),
                           preferred_element_type=F32)


def _split(x):
    hi = x.astype(BF16)
    return hi, (x - hi.astype(F32)).astype(BF16)


def _dot_exact_rhs(a, b01):
    hi, lo = _split(a)
    return (jnp.dot(hi, b01, preferred_element_type=F32)
            + jnp.dot(lo, b01, preferred_element_type=F32))


def _cumsum_rows(tri01, x):
    hi, lo = _split(x)
    return (jnp.dot(tri01, hi, preferred_element_type=F32)
            + jnp.dot(tri01, lo, preferred_element_type=F32))


def _dot_hilo(a, bh, bl):
    ah, al = _split(a)
    return (jnp.dot(ah, bh, preferred_element_type=F32)
            + jnp.dot(ah, bl, preferred_element_type=F32)
            + jnp.dot(al, bh, preferred_element_type=F32))


def _sigmoid(x):
    return 1.0 / (1.0 + jnp.exp(-x))


def _silu(x):
    return x * _sigmoid(x)


def _softplus(x):
    return jnp.maximum(x, 0.0) + jnp.log1p(jnp.exp(-jnp.abs(x)))


def _log_sigmoid(x):
    return -_softplus(-x)


def _tril01(n):
    r = lax.broadcasted_iota(jnp.int32, (n, n), 0)
    c = lax.broadcasted_iota(jnp.int32, (n, n), 1)
    return r >= c


def _norm_matmul_kernel(x_ref, g_ref, w_ref, *o_refs, segs):
    x = x_ref[...]
    h = x * lax.rsqrt(jnp.mean(x * x, axis=-1, keepdims=True) + EPS)
    h = (h * g_ref[...]).astype(BF16)
    c0 = 0
    for o_ref, width in zip(o_refs, segs):
        for j in range(0, width, COL_CHUNK):
            w = min(COL_CHUNK, width - j)
            o_ref[:, j:j + w] = jnp.dot(h, w_ref[:, c0 + j:c0 + j + w],
                                        preferred_element_type=F32).astype(o_ref.dtype)
        c0 += width


def _norm_matmul(x, g, w, segs, dtypes, tm=TOKEN_TILE):
    M, D = x.shape
    tm = min(tm, M)
    assert M % tm == 0 and w.shape == (D, sum(segs)) and all(s % LANE == 0 for s in segs)
    return pl.pallas_call(
        functools.partial(_norm_matmul_kernel, segs=tuple(segs)),
        grid=(M // tm,),
        in_specs=[pl.BlockSpec((tm, D), lambda i: (i, 0)),
                  pl.BlockSpec((1, D), lambda i: (0, 0)),
                  pl.BlockSpec(w.shape, lambda i: (0, 0))],
        out_specs=[pl.BlockSpec((tm, s), lambda i: (i, 0)) for s in segs],
        out_shape=[jax.ShapeDtypeStruct((M, s), dt) for s, dt in zip(segs, dtypes)],
        compiler_params=_params("parallel"),
        name="norm_matmul",
    )(x, g.reshape(1, D).astype(F32), w)


def _gla_chunk_heads(q, k, v, la, tri01, tril, st_ref, H, KP, VP):
    C = q.shape[0]
    b = _cumsum_rows(tri01, la)
    outs = []
    for h in range(H):
        ks = slice(h * KP, (h + 1) * KP)
        bh = b[:, ks]
        bl = b[C - 1:C, ks]
        br = b[C // 2 - 1:C // 2, ks]
        vh = v[:, h * VP:(h + 1) * VP]
        qe = q[:, ks] * jnp.exp(bh - br)
        ke = k[:, ks] * jnp.exp(br - bh)
        att = jnp.where(tril, _dot_nt(qe, ke), 0.0)
        st = st_ref[h]
        o = _dot(att, vh) + _dot_nt(qe * jnp.exp(br), st)
        st_ref[h] = jnp.exp(bl) * st + _dot_tn(vh, ke * jnp.exp(bl - br))
        outs.append(o)
    return outs


def _head_rms_gate(o, g, gate, dv):
    y = o * lax.rsqrt(jnp.sum(o * o, axis=-1, keepdims=True) * (1.0 / dv) + EPS)
    return (y * g) * gate


def _gla_kernel(q_ref, k_ref, v_ref, og_ref, glr_ref, wg_ref, bg_ref, on_ref, o_ref, st_ref,
                *, H, KP, VP, dk, dv, T):
    @pl.when(pl.program_id(1) == 0)
    def _():
        st_ref[...] = jnp.zeros_like(st_ref)

    tril = _tril01(CHUNK)
    tri01 = tril.astype(BF16)
    wgh, wgl = _split(wg_ref[...])
    bg = bg_ref[...]
    on = on_ref[...]

    def chunk(c, carry):
        rows = pl.ds(pl.multiple_of(c * CHUNK, CHUNK), CHUNK)
        la = _log_sigmoid(_dot_hilo(glr_ref[rows, :], wgh, wgl) + bg) * (1.0 / GLA_GATE_NORM)
        q = q_ref[rows, :] * (dk ** -0.5)
        outs = _gla_chunk_heads(q, k_ref[rows, :], v_ref[rows, :], la, tri01, tril, st_ref,
                                H, KP, VP)
        og = og_ref[rows, :]
        for h, o in enumerate(outs):
            vs = slice(h * VP, (h + 1) * VP)
            o_ref[rows, vs] = _head_rms_gate(o, on, _silu(og[:, vs]), dv).astype(o_ref.dtype)
        return carry

    lax.fori_loop(0, T // CHUNK, chunk, 0, unroll=2)


def _hgrn_kernel(q_ref, f_ref, v_ref, og_ref, lb_ref, on_ref, o_ref, st_ref, *, H, KP, VP, dv, T):
    @pl.when(pl.program_id(1) == 0)
    def _():
        st_ref[...] = jnp.zeros_like(st_ref)

    tril = _tril01(CHUNK)
    tri01 = tril.astype(BF16)
    lb = lb_ref[...]
    on = on_ref[...]

    def chunk(c, carry):
        rows = pl.ds(pl.multiple_of(c * CHUNK, CHUNK), CHUNK)
        fg = lb + (1.0 - lb) * _sigmoid(f_ref[rows, :])
        q = _silu(q_ref[rows, :])
        outs = _gla_chunk_heads(q, 1.0 - fg, v_ref[rows, :], jnp.log(fg), tri01, tril, st_ref,
                                H, KP, VP)
        og = og_ref[rows, :]
        for h, o in enumerate(outs):
            vs = slice(h * VP, (h + 1) * VP)
            o_ref[rows, vs] = _head_rms_gate(o, on, _sigmoid(og[:, vs]), dv).astype(o_ref.dtype)
        return carry

    lax.fori_loop(0, T // CHUNK, chunk, 0, unroll=2)


def _seq_spec(T, width, col=0):
    return pl.BlockSpec((None, T, width), lambda b, t: (b, t, col))


def _const_spec(shape):
    return pl.BlockSpec(shape, lambda b, t: (0,) * len(shape))


def _gla_mixer(q, k, v, og, glr, wg, bg, on, *, H, KP, VP, dk, dv):
    B, S, _ = q.shape
    T = min(SEQ_TILE, S)
    assert S % T == 0 and T % CHUNK == 0
    return pl.pallas_call(
        functools.partial(_gla_kernel, H=H, KP=KP, VP=VP, dk=dk, dv=dv, T=T),
        grid=(B, S // T),
        in_specs=[_seq_spec(T, H * KP), _seq_spec(T, H * KP), _seq_spec(T, H * VP),
                  _seq_spec(T, H * VP), _seq_spec(T, LANE),
                  _const_spec(wg.shape), _const_spec(bg.shape), _const_spec(on.shape)],
        out_specs=_seq_spec(T, H * VP),
        out_shape=jax.ShapeDtypeStruct((B, S, H * VP), BF16),
        scratch_shapes=[pltpu.VMEM((H, VP, KP), F32)],
        compiler_params=_params("parallel", "arbitrary"),
        name="gla_mixer",
    )(q, k, v, og, glr, wg, bg, on)


def _hgrn_mixer(q, f, v, og, lb, on, *, H, KP, VP, dv):
    B, S, _ = q.shape
    T = min(SEQ_TILE, S)
    assert S % T == 0 and T % CHUNK == 0
    return pl.pallas_call(
        functools.partial(_hgrn_kernel, H=H, KP=KP, VP=VP, dv=dv, T=T),
        grid=(B, S // T),
        in_specs=[_seq_spec(T, H * KP), _seq_spec(T, H * KP), _seq_spec(T, H * VP),
                  _seq_spec(T, H * VP), _const_spec(lb.shape), _const_spec(on.shape)],
        out_specs=_seq_spec(T, H * VP),
        out_shape=jax.ShapeDtypeStruct((B, S, H * VP), BF16),
        scratch_shapes=[pltpu.VMEM((H, VP, KP), F32)],
        compiler_params=_params("parallel", "arbitrary"),
        name="hgrn_mixer",
    )(q, f, v, og, lb, on)


def _ssd_kernel(z_ref, xbc_ref, dt_ref, cw_ref, cb_ref, dtb_ref, alog_ref, dsk_ref, ng_ref, e_ref,
                o_ref, carry_ref, xc_ref, st_ref, *, T, DM, N, G):
    @pl.when(pl.program_id(1) == 0)
    def _():
        st_ref[...] = jnp.zeros_like(st_ref)
        carry_ref[...] = jnp.zeros_like(carry_ref)

    C = CHUNK
    GW = DM // G
    tril = _tril01(C)
    tri01 = tril.astype(BF16)
    e01 = e_ref[...]
    a_neg = -jnp.exp(alog_ref[...])
    dtb = dtb_ref[...]
    cbias = cb_ref[...]
    dsk = dsk_ref[...]
    ng = ng_ref[...]
    P = SSM_HEAD_DIM
    rr = lax.broadcasted_iota(jnp.int32, (C, DM), 0)
    cc = lax.broadcasted_iota(jnp.int32, (C, DM), 1)
    diag_sel = (cc & (P - 1)) == rr
    r2 = lax.broadcasted_iota(jnp.int32, (C, 2 * P), 0)
    c2 = lax.broadcasted_iota(jnp.int32, (C, 2 * P), 1)
    causal2 = r2 >= (c2 & (P - 1))
    lo_half = lax.broadcasted_iota(jnp.int32, (C, 2 * P), 1) < P

    cw = cw_ref[...]
    for r0 in range(0, T, C):
        prev = carry_ref[...] if r0 == 0 else xbc_ref[r0 - SUBLANE:r0, :]
        xc_ref[r0:r0 + C, :] = _silu(_causal_taps(xbc_ref[r0:r0 + C, :], prev, cw, cbias))
    carry_ref[...] = xbc_ref[T - SUBLANE:T, :]

    def chunk(c, carry):
        rows = pl.ds(pl.multiple_of(c * C, C), C)
        xc = xc_ref[rows, :]
        xs = xc[:, :DM]
        dt = _softplus(dt_ref[rows, :] + dtb)
        acs = _cumsum_rows(tri01, dt * a_neg)
        dt_e = _dot_exact_rhs(dt, e01)
        acs_e = _dot_exact_rhs(acs, e01)
        acs_row = jnp.sum(jnp.where(diag_sel, acs_e, 0.0), axis=0, keepdims=True)
        acs_last = acs_e[C - 1:C, :]
        xdt = xs * dt_e
        z = z_ref[rows, :]
        for g in range(G):
            gs = slice(g * GW, (g + 1) * GW)
            bm = xc[:, DM + g * N:DM + (g + 1) * N]
            cm = xc[:, DM + G * N + g * N:DM + G * N + (g + 1) * N]
            cb2 = _dot_nt(cm, jnp.concatenate([bm, bm], axis=0))
            yd = []
            for p in range(GW // (2 * P)):
                ps = slice(g * GW + p * 2 * P, g * GW + (p + 1) * 2 * P)
                seg = acs_e[:, ps] - acs_row[:, ps]
                lmat = jnp.exp(jnp.where(causal2, seg, -jnp.inf))
                xp = xdt[:, ps]
                rhs = jnp.concatenate([jnp.where(lo_half, xp, 0.0), jnp.where(lo_half, 0.0, xp)],
                                      axis=0)
                yd.append(_dot(cb2 * lmat, rhs))
            y = jnp.concatenate(yd, axis=1)
            st = st_ref[g]
            y = y + _dot(cm, st) * jnp.exp(acs_e[:, gs])
            xe = xdt[:, gs] * jnp.exp(acs_last[:, gs] - acs_e[:, gs])
            st_ref[g] = jnp.exp(acs_last[:, gs]) * st + _dot_tn(bm, xe)
            y = y + dsk[:, gs] * xs[:, gs]
            y = y * _silu(z[:, gs])
            y = y * lax.rsqrt(jnp.mean(y * y, axis=-1, keepdims=True) + EPS)
            o_ref[rows, gs] = (y * ng[:, gs]).astype(o_ref.dtype)
        return carry

    lax.fori_loop(0, T // C, chunk, 0, unroll=2)


def _ssd_mixer(z, xbc, dt, cw, cb, dtb, alog, dsk, ng, e01):
    B, S, DM = z.shape
    W = xbc.shape[-1]
    T = min(SEQ_TILE, S)
    assert S % T == 0 and T % CHUNK == 0
    G, N = SSM_GROUPS, SSM_STATE
    return pl.pallas_call(
        functools.partial(_ssd_kernel, T=T, DM=DM, N=N, G=G),
        grid=(B, S // T),
        in_specs=[_seq_spec(T, DM), _seq_spec(T, W), _seq_spec(T, LANE),
                  _const_spec(cw.shape), _const_spec(cb.shape), _const_spec(dtb.shape),
                  _const_spec(alog.shape), _const_spec(dsk.shape), _const_spec(ng.shape),
                  _const_spec(e01.shape)],
        out_specs=_seq_spec(T, DM),
        out_shape=jax.ShapeDtypeStruct((B, S, DM), BF16),
        scratch_shapes=[pltpu.VMEM((SUBLANE, W), F32), pltpu.VMEM((T, W), F32),
                        pltpu.VMEM((G, N, DM // G), F32)],
        compiler_params=_params("parallel", "arbitrary"),
        name="ssd_mixer",
    )(z, xbc, dt, cw, cb, dtb, alog, dsk, ng, e01)


def _dil_in_proj_kernel(x_ref, g_ref, w_ref, pos_ref, qn_ref, kn_ref, invf_ref, sign_ref, *refs,
                        dils, NH, HD):
    NG = len(dils)
    GW = NH * HD
    outs, xq_ref, buf_ref = refs[:3 * NG], refs[3 * NG], refs[3 * NG + 1]
    tm = x_ref.shape[0]
    x = x_ref[...]
    h = x * lax.rsqrt(jnp.mean(x * x, axis=-1, keepdims=True) + EPS)
    h = (h * g_ref[...]).astype(BF16)
    ang = pos_ref[...].astype(F32) * invf_ref[...]
    cos = jnp.cos(ang)
    sin = jnp.sin(ang) * sign_ref[...]
    gains = (qn_ref[...], kn_ref[...])
    for kind in range(3):
        for g, r in enumerate(dils):
            c0 = (kind * NG + g) * GW
            u = jnp.dot(h, w_ref[:, c0:c0 + GW], preferred_element_type=F32)
            o_ref = outs[kind * NG + g]
            for hh in range(NH):
                hs = slice(hh * HD, (hh + 1) * HD)
                y = u[:, hs]
                if kind < 2:
                    y = (y * lax.rsqrt(jnp.mean(y * y, axis=-1, keepdims=True) + EPS)) * gains[kind]
                    y = y * cos + pltpu.roll(y, HD // 2, 1) * sin
                if r == 1:
                    o_ref[0, :, hs] = y.astype(o_ref.dtype)
                else:
                    slab = ((kind * NG + g) % 2) * NH + hh
                    buf_ref[slab] = y
                    for rho in range(r):
                        o_ref[rho, :, hs] = buf_ref[slab, pl.ds(rho, tm // r, stride=r), :].astype(
                            o_ref.dtype)
    xq_ref[...] = jnp.dot(h, w_ref[:, 3 * NG * GW:], preferred_element_type=F32)


def _dil_in_proj(x, g, w, pos, qn, kn, dils):
    B, S, D = x.shape
    NH, HD = DIL_HEADS, DIL_HEAD_DIM
    GW = NH * HD
    NG = len(dils)
    T = min(TOKEN_TILE, S)
    WQ = w.shape[1] - 3 * NG * GW
    assert S % T == 0 and all(T % (r * 2 * SUBLANE) == 0 for r in dils)
    half = HD // 2
    inv_freq = ROPE_THETA ** (-jnp.arange(half, dtype=F32) / half)
    invf = jnp.concatenate([inv_freq, inv_freq]).reshape(1, HD)
    sign = jnp.concatenate([-jnp.ones((half,), F32), jnp.ones((half,), F32)]).reshape(1, HD)
    grp_spec = [pl.BlockSpec((None, r, T // r, GW), lambda b, t: (b, 0, t, 0)) for r in dils]
    grp_shape = [jax.ShapeDtypeStruct((B, r, S // r, GW), BF16) for r in dils]
    res = pl.pallas_call(
        functools.partial(_dil_in_proj_kernel, dils=tuple(dils), NH=NH, HD=HD),
        grid=(B, S // T),
        in_specs=[_seq_spec(T, D), _const_spec((1, D)),
                  pl.BlockSpec(w.shape, lambda b, t: (0, 0), pipeline_mode=pl.Buffered(1)),
                  _seq_spec(T, 1), _const_spec((1, HD)), _const_spec((1, HD)),
                  _const_spec((1, HD)), _const_spec((1, HD))],
        out_specs=grp_spec * 3 + [_seq_spec(T, WQ)],
        out_shape=grp_shape * 3 + [jax.ShapeDtypeStruct((B, S, WQ), F32)],
        scratch_shapes=[pltpu.VMEM((2 * NH, T, HD), F32)],
        compiler_params=_params("parallel", "parallel"),
        name="dilated_in_proj",
    )(x, g.reshape(1, D).astype(F32), w, pos, qn.reshape(1, HD), kn.reshape(1, HD), invf, sign)
    return res[:NG], res[NG:2 * NG], res[2 * NG:3 * NG], res[3 * NG]


def _dil_attn_kernel(q_ref, kp_ref, kc_ref, vp_ref, vc_ref, o_ref, lse_ref, *, nq, W, NH, HD):
    n0 = pl.program_id(2)
    Q = DIL_BLOCK
    i_idx = lax.broadcasted_iota(jnp.int32, (Q, 2 * Q), 0)
    j_idx = lax.broadcasted_iota(jnp.int32, (Q, 2 * Q), 1)
    dist = Q + i_idx - j_idx
    band = (dist >= 0) & (dist <= W)
    jmin = jnp.where(n0 > 0, 0, Q)
    band_first = band & (j_idx >= jmin)
    lane = lax.broadcasted_iota(jnp.int32, (Q, LANE), 1)
    scale = HD ** -0.5
    for i in range(nq):
        rows = slice(i * Q, (i + 1) * Q)
        mask = band_first if i == 0 else band
        lse_tile = jnp.zeros((Q, LANE), F32)
        for h in range(NH):
            hs = slice(h * HD, (h + 1) * HD)
            if i == 0:
                kk = jnp.concatenate([kp_ref[:, hs], kc_ref[0:Q, hs]], axis=0)
                vv = jnp.concatenate([vp_ref[:, hs], vc_ref[0:Q, hs]], axis=0)
            else:
                kk = kc_ref[(i - 1) * Q:(i + 1) * Q, hs]
                vv = vc_ref[(i - 1) * Q:(i + 1) * Q, hs]
            s = _dot_nt(q_ref[rows, hs], kk) * scale
            s = jnp.where(mask, s, -jnp.inf)
            m = jnp.max(s, axis=-1, keepdims=True)
            p = jnp.exp(s - m)
            l = jnp.sum(p, axis=-1, keepdims=True)
            o_ref[rows, hs] = _dot(p, vv) * (1.0 / l)
            lse_tile = jnp.where(lane == h, m + jnp.log(l), lse_tile)
        lse_ref[rows, :] = lse_tile


def _dil_attention(q, k, v, window, dilation):
    B, r, L, GW = q.shape
    NH, HD, Q = DIL_HEADS, DIL_HEAD_DIM, DIL_BLOCK
    W = window // dilation
    assert r == dilation and L % Q == 0 and W <= Q
    nq = min(4, L // Q)
    assert (L // Q) % nq == 0
    cur = pl.BlockSpec((None, None, nq * Q, GW), lambda b, rho, n: (b, rho, n, 0))
    prev = pl.BlockSpec((None, None, Q, GW),
                        lambda b, rho, n: (b, rho, jnp.maximum(n * nq - 1, 0), 0))
    return pl.pallas_call(
        functools.partial(_dil_attn_kernel, nq=nq, W=W, NH=NH, HD=HD),
        grid=(B, r, L // (nq * Q)),
        in_specs=[cur, prev, cur, prev, cur],
        out_specs=[cur, pl.BlockSpec((None, None, nq * Q, LANE), lambda b, rho, n: (b, rho, n, 0))],
        out_shape=[jax.ShapeDtypeStruct((B, r, L, GW), F32),
                   jax.ShapeDtypeStruct((B, r, L, LANE), F32)],
        compiler_params=_params("parallel", "parallel", "arbitrary"),
        name=f"dilated_attention_r{r}",
    )(q, k, k, v, v)


def _dil_merge_kernel(*refs, dils, NH, HD):
    NG = len(dils)
    o_refs, l_refs, out_ref, obuf_ref, lbuf_ref = (refs[:NG], refs[NG:2 * NG], refs[2 * NG],
                                                   refs[2 * NG + 1], refs[2 * NG + 2])
    tm = out_ref.shape[0]
    ls = []
    for g, r in enumerate(dils):
        if r == 1:
            ls.append(l_refs[g][0])
            continue
        for rho in range(r):
            rows = pl.ds(rho, tm // r, stride=r)
            lbuf_ref[g, rows, :] = l_refs[g][rho]
            for h in range(NH):
                obuf_ref[g * NH + h, rows, :] = o_refs[g][rho, :, h * HD:(h + 1) * HD]
        ls.append(lbuf_ref[g])
    m = functools.reduce(jnp.maximum, ls)
    es = [jnp.exp(l - m) for l in ls]
    inv = 1.0 / functools.reduce(lambda a, b: a + b, es)
    for h in range(NH):
        hs = slice(h * HD, (h + 1) * HD)
        acc = None
        for g, r in enumerate(dils):
            o = o_refs[g][0, :, hs] if r == 1 else obuf_ref[g * NH + h]
            term = (es[g][:, h:h + 1] * inv[:, h:h + 1]) * o
            acc = term if acc is None else acc + term
        out_ref[:, hs] = acc.astype(out_ref.dtype)


def _dil_merge(outs, lses, dils):
    B, _, L0, GW = outs[0].shape
    S = L0 * dils[0]
    NH, HD = DIL_HEADS, DIL_HEAD_DIM
    NG = len(dils)
    T = min(TOKEN_TILE, S)
    grp = lambda w: [pl.BlockSpec((None, r, T // r, w), lambda b, t: (b, 0, t, 0)) for r in dils]
    return pl.pallas_call(
        functools.partial(_dil_merge_kernel, dils=tuple(dils), NH=NH, HD=HD),
        grid=(B, S // T),
        in_specs=grp(GW) + grp(LANE),
        out_specs=_seq_spec(T, GW),
        out_shape=jax.ShapeDtypeStruct((B, S, GW), BF16),
        scratch_shapes=[pltpu.VMEM((NG * NH, T, HD), F32), pltpu.VMEM((NG, T, LANE), F32)],
        compiler_params=_params("parallel", "parallel"),
        name="dilated_merge",
    )(*outs, *lses)


def _pair_rms(x, lo, hd):
    sq = x * x
    s_lo = jnp.sum(jnp.where(lo, sq, 0.0), axis=-1, keepdims=True)
    s_hi = jnp.sum(jnp.where(lo, 0.0, sq), axis=-1, keepdims=True)
    r = jnp.where(lo, lax.rsqrt(s_lo * (1.0 / hd) + EPS), lax.rsqrt(s_hi * (1.0 / hd) + EPS))
    return x * r


def _memory_xattn(xq_ref, kv_ref, qn2, kn2, hd):
    T, W = xq_ref.shape
    M = kv_ref.shape[0]
    assert 2 * hd == LANE and W % LANE == 0
    lo_q = lax.broadcasted_iota(jnp.int32, (T, LANE), 1) < hd
    lo_k = lax.broadcasted_iota(jnp.int32, (M, LANE), 1) < hd
    scale = hd ** -0.5
    outs = []
    for p in range(W // LANE):
        ps = slice(p * LANE, (p + 1) * LANE)
        q = _pair_rms(xq_ref[:, ps], lo_q, hd) * qn2
        k = (_pair_rms(kv_ref[:, ps], lo_k, hd) * kn2).astype(BF16)
        v = kv_ref[:, W + p * LANE:W + (p + 1) * LANE]
        o = None
        for first in (True, False):
            qm = jnp.where(lo_q, q, 0.0) if first else jnp.where(lo_q, 0.0, q)
            vm = jnp.where(lo_k, v, 0.0) if first else jnp.where(lo_k, 0.0, v)
            s = _dot_nt(qm, k) * scale
            m = jnp.max(s, axis=-1, keepdims=True)
            e = jnp.exp(s - m)
            l = jnp.sum(e, axis=-1, keepdims=True)
            t = _dot(e, vm) * (1.0 / l)
            o = t if o is None else o + t
        outs.append(o)
    return jnp.concatenate(outs, axis=1).astype(BF16)


def _causal_taps(u, prev, cw, cb):
    K = cw.shape[0]
    sub = lax.broadcasted_iota(jnp.int32, (SUBLANE, u.shape[1]), 0)
    acc = cb + cw[K - 1:K, :] * u
    for k in range(1, K):
        rolled = pltpu.roll(u, k, 0)
        head = jnp.where(sub < k, pltpu.roll(prev, k, 0), rolled[0:SUBLANE, :])
        shifted = jnp.concatenate([head, rolled[SUBLANE:, :]], axis=0)
        acc = acc + cw[K - 1 - k:K - k, :] * shifted
    return acc


def _tail_kernel(x_ref, tok_ref, xq_ref, kv_ref, qn_ref, kn_ref, wt_ref, wx_ref, g_ref, wu_ref,
                 cw_ref, cb_ref, wd_ref, o_ref, carry_ref, act_ref, x1_ref, *, T, FF, FC):
    @pl.when(pl.program_id(1) == 0)
    def _():
        carry_ref[...] = jnp.zeros_like(carry_ref)

    D = o_ref.shape[1]
    xa = _memory_xattn(xq_ref, kv_ref, qn_ref[...], kn_ref[...], XA_HEAD_DIM)
    tok = tok_ref[...]
    for n in range(0, D, COL_CHUNK):
        cs = slice(n, n + COL_CHUNK)
        x1_ref[:, cs] = (x_ref[:, cs] + jnp.dot(tok, wt_ref[:, cs], preferred_element_type=F32)
                         + jnp.dot(xa, wx_ref[:, cs], preferred_element_type=F32))

    x = x1_ref[...]
    h = x * lax.rsqrt(jnp.mean(x * x, axis=-1, keepdims=True) + EPS)
    h = (h * g_ref[...]).astype(BF16)

    def conv(cols):
        u = jnp.dot(h, wu_ref[:, cols], preferred_element_type=F32)
        prev = carry_ref[:, cols]
        carry_ref[:, cols] = u[T - SUBLANE:T, :]
        return _causal_taps(u, prev, cw_ref[:, cols], cb_ref[:, cols])

    for j in range(0, FF, FC):
        gate = conv(slice(j, j + FC))
        val = conv(slice(FF + j, FF + j + FC))
        act_ref[:, j:j + FC] = (_silu(gate) * val).astype(BF16)
    for n in range(0, D, COL_CHUNK):
        cs = slice(n, n + COL_CHUNK)
        o_ref[:, cs] = x1_ref[:, cs] + jnp.dot(act_ref[...], wd_ref[:, cs],
                                               preferred_element_type=F32)


def _layer_tail(x, tok, xq, kv_all, layer, qn2, kn2, wt, wx, g, wu, cw, cb, wd):
    B, S, D = x.shape
    FF = wd.shape[0]
    WQ = xq.shape[-1]
    Mm = kv_all.shape[1]
    T = min(SEQ_TILE, S)
    FC = 256
    assert FF % FC == 0 and S % T == 0 and D % COL_CHUNK == 0
    resident = lambda a: pl.BlockSpec(a.shape, lambda b, t: (0, 0), pipeline_mode=pl.Buffered(1))
    return pl.pallas_call(
        functools.partial(_tail_kernel, T=T, FF=FF, FC=FC),
        grid=(B, S // T),
        in_specs=[_seq_spec(T, D), _seq_spec(T, tok.shape[-1]), _seq_spec(T, WQ),
                  pl.BlockSpec((None, Mm, 2 * WQ), lambda b, t: (b, 0, layer)),
                  _const_spec(qn2.shape), _const_spec(kn2.shape), resident(wt), resident(wx),
                  _const_spec((1, D)), resident(wu), _const_spec(cw.shape), _const_spec(cb.shape),
                  resident(wd)],
        out_specs=_seq_spec(T, D),
        out_shape=jax.ShapeDtypeStruct((B, S, D), F32),
        scratch_shapes=[pltpu.VMEM((SUBLANE, 2 * FF), F32), pltpu.VMEM((T, FF), BF16),
                        pltpu.VMEM((T, D), F32)],
        compiler_params=_params("parallel", "arbitrary"),
        name="layer_tail",
    )(x, tok, xq, kv_all, qn2, kn2, wt, wx, g.reshape(1, D), wu, cw, cb, wd)


def _pad_heads_cols(w, H, d, dp):
    R = w.shape[0]
    return jnp.pad(w.reshape(R, H, d), ((0, 0), (0, 0), (0, dp - d))).reshape(R, H * dp)


def _pad_heads_rows(w, H, d, dp):
    C = w.shape[1]
    return jnp.pad(w.reshape(H, d, C), ((0, 0), (0, dp - d), (0, 0))).reshape(H * dp, C)


def _pad_cols(w, n):
    return jnp.pad(w, ((0, 0), (0, n - w.shape[1])))


def _cols(w, sizes):
    out, c = [], 0
    for s in sizes:
        out.append(w[:, c:c + s])
        c += s
    return out


def _out_weights(w_out, d_tok, tok_pad=None):
    wt, wx = w_out[:d_tok], w_out[d_tok:]
    if tok_pad is not None:
        wt = _pad_heads_rows(wt, *tok_pad)
    return wt.astype(BF16), wx.astype(BF16)


def _gla_layer(xf, norm_g, B, S, w_in, w_gate2, b_gate, o_norm):
    D = xf.shape[1]
    d_mix = 3 * D // 4
    H = GLA_HEADS
    dk, dv = d_mix // 2 // H, d_mix // H
    KP, VP = -(-dk // LANE) * LANE, -(-dv // LANE) * LANE
    q, k, v, glr, og, xq = _cols(w_in, [H * dk, H * dk, d_mix, GLA_RANK, d_mix, XA_HEADS * XA_HEAD_DIM])
    w = jnp.concatenate([_pad_heads_cols(q, H, dk, KP), _pad_heads_cols(k, H, dk, KP),
                         _pad_heads_cols(v, H, dv, VP), _pad_heads_cols(og, H, dv, VP),
                         xq, _pad_cols(glr, LANE)], axis=1).astype(BF16)
    segs = [H * KP, H * KP, H * VP, H * VP, XA_HEADS * XA_HEAD_DIM, LANE]
    q, k, v, og, xq, glr = _norm_matmul(xf, norm_g, w, segs, [F32, F32, BF16, F32, F32, F32])
    wg = jnp.pad(_pad_heads_cols(w_gate2, H, dk, KP), ((0, LANE - GLA_RANK), (0, 0)))
    bg = _pad_heads_cols(b_gate.reshape(1, -1), H, dk, KP)
    on = _pad_cols(o_norm.reshape(1, -1), VP)
    r3 = lambda a: a.reshape(B, S, -1)
    tok = _gla_mixer(r3(q), r3(k), r3(v), r3(og), r3(glr), wg, bg, on,
                     H=H, KP=KP, VP=VP, dk=dk, dv=dv)
    return tok, xq, (H, dv, VP)


def _dilated_layer(xf, norm_g, B, S, w_in, q_norm, k_norm, positions):
    NG = len(DIL_GROUPS)
    GW = DIL_HEADS * DIL_HEAD_DIM
    dils = [d for _, d in DIL_GROUPS]
    q, k, v, xq = _cols(w_in, [NG * GW] * 3 + [XA_HEADS * XA_HEAD_DIM])
    w = jnp.concatenate([q, k, v, xq], axis=1).astype(BF16)
    qs, ks, vs, xq = _dil_in_proj(xf.reshape(B, S, -1), norm_g, w, positions.reshape(B, S, 1),
                                  q_norm, k_norm, dils)
    outs, lses = [], []
    for g, (window, dilation) in enumerate(DIL_GROUPS):
        o, lse = _dil_attention(qs[g], ks[g], vs[g], window, dilation)
        outs.append(o)
        lses.append(lse)
    tok = _dil_merge(outs, lses, dils)
    return tok, xq.reshape(B * S, -1), None


def _mamba_layer(xf, norm_g, B, S, w_in, conv_w, conv_b, dt_bias, a_log, d_skip, norm_gain):
    D = xf.shape[1]
    d_mix = 3 * D // 4
    GN = SSM_GROUPS * SSM_STATE
    NHD = d_mix // SSM_HEAD_DIM
    z, xbc, dt, xq = _cols(w_in, [d_mix, d_mix + 2 * GN, NHD, XA_HEADS * XA_HEAD_DIM])
    w = jnp.concatenate([z, xbc, _pad_cols(dt, LANE), xq], axis=1).astype(BF16)
    z, xbc, dt, xq = _norm_matmul(xf, norm_g, w, [d_mix, d_mix + 2 * GN, LANE, XA_HEADS * XA_HEAD_DIM],
                                  [F32, F32, F32, F32])
    padl = lambda a: _pad_cols(a.reshape(1, -1).astype(F32), LANE)
    head = lax.broadcasted_iota(jnp.int32, (LANE, d_mix), 0)
    col = lax.broadcasted_iota(jnp.int32, (LANE, d_mix), 1)
    e01 = (col // SSM_HEAD_DIM == head).astype(BF16)
    r3 = lambda a: a.reshape(B, S, -1)
    tok = _ssd_mixer(r3(z), r3(xbc), r3(dt), conv_w, conv_b.reshape(1, -1), padl(dt_bias),
                     padl(a_log), jnp.repeat(d_skip, SSM_HEAD_DIM).reshape(1, -1),
                     norm_gain.reshape(1, -1), e01)
    return tok, xq, None


def _hgrn_layer(xf, norm_g, B, S, w_in, lower_bounds, o_norm, layer):
    D = xf.shape[1]
    d_mix = 3 * D // 4
    H = d_mix // HGRN_EXPAND
    KP, VP = HGRN_EXPAND, d_mix // H
    q, f, i, og, xq = _cols(w_in, [H * KP, H * KP, d_mix, d_mix, XA_HEADS * XA_HEAD_DIM])
    w = jnp.concatenate([q, f, i, og, xq], axis=1).astype(BF16)
    q, f, i, og, xq = _norm_matmul(xf, norm_g, w, [H * KP, H * KP, d_mix, d_mix, XA_HEADS * XA_HEAD_DIM],
                                   [F32, F32, BF16, F32, F32])
    lbs = jnp.cumsum(jax.nn.softmax(lower_bounds.astype(F32), axis=0), axis=0)
    lb = (lbs[layer] - lbs[0]).reshape(1, -1)
    r3 = lambda a: a.reshape(B, S, -1)
    tok = _hgrn_mixer(r3(q), r3(f), r3(i), r3(og), lb, o_norm.reshape(1, -1),
                      H=H, KP=KP, VP=VP, dv=VP)
    return tok, xq, None


def kernel(x, mem, positions, mem_norm, mix_norm, xa_w_kv, xa_q_norm, xa_k_norm, ffn_norm, ffn_w_up, ffn_conv_w, ffn_conv_b, ffn_w_down, a_w_in, a_w_gate2, a_b_gate, a_o_norm, a_w_out, b_w_in, b_q_norm, b_k_norm, b_w_out, c_w_in, c_conv_w, c_conv_b, c_dt_bias, c_a_log, c_d, c_norm, c_w_out, d_w_in, d_lower_bounds, d_o_norm, d_w_out):
    B, S, D = x.shape
    depth = mix_norm.shape[0]
    Mm = mem.shape[1]
    wkv = jnp.concatenate([xa_w_kv[i] for i in range(depth)], axis=1).astype(BF16)
    (kv_all,) = _norm_matmul(mem.reshape(B * Mm, D), mem_norm, wkv, [wkv.shape[1]], [F32])
    kv_all = kv_all.reshape(B, Mm, -1)
    pair = lambda g: jnp.concatenate([g, g]).reshape(1, -1)

    xf = x.reshape(B * S, D)
    for i in range(depth):
        kind = i % N_MIXERS
        if kind == 0:
            tok, xq, tok_pad = _gla_layer(xf, mix_norm[i], B, S, a_w_in, a_w_gate2, a_b_gate, a_o_norm)
            w_out, d_tok = a_w_out, 3 * D // 4
        elif kind == 1:
            tok, xq, tok_pad = _dilated_layer(xf, mix_norm[i], B, S, b_w_in, b_q_norm, b_k_norm, positions)
            w_out, d_tok = b_w_out, DIL_HEADS * DIL_HEAD_DIM
        elif kind == 2:
            tok, xq, tok_pad = _mamba_layer(xf, mix_norm[i], B, S, c_w_in, c_conv_w, c_conv_b,
                                            c_dt_bias, c_a_log, c_d, c_norm)
            w_out, d_tok = c_w_out, 3 * D // 4
        else:
            tok, xq, tok_pad = _hgrn_layer(xf, mix_norm[i], B, S, d_w_in, d_lower_bounds, d_o_norm, i)
            w_out, d_tok = d_w_out, 3 * D // 4
        wt, wx = _out_weights(w_out, d_tok, tok_pad)
        xf = _layer_tail(xf.reshape(B, S, D), tok.reshape(B, S, -1), xq.reshape(B, S, -1), kv_all, i,
                         pair(xa_q_norm[i]), pair(xa_k_norm[i]), wt, wx, ffn_norm[i],
                         ffn_w_up[i].astype(BF16), ffn_conv_w[i], ffn_conv_b[i].reshape(1, -1),
                         ffn_w_down[i].astype(BF16)).reshape(B * S, D)
    return xf.reshape(B, S, D)
```

```python
import functools
import math

import jax
import jax.numpy as jnp
from jax import lax
from jax.experimental import pallas as pl
from jax.experimental.pallas import tpu as pltpu

F32 = jnp.float32
BF16 = jnp.bfloat16

LANE = 128
SUBLANE = 8
VMEM_LIMIT = 56 * 1024 * 1024

EPS = 1e-6
ROPE_THETA = 10000.0
CHUNK = 64
N_MIXERS = 4
XA_HEADS, XA_HEAD_DIM = 4, 64
GLA_HEADS, GLA_RANK, GLA_GATE_NORM = 4, 16, 16.0
DIL_GROUPS = ((128, 1), (512, 4), (2048, 16))
DIL_HEADS, DIL_HEAD_DIM, DIL_BLOCK = 4, 128, 128
SSM_HEAD_DIM, SSM_GROUPS, SSM_STATE = 64, 2, 128
HGRN_EXPAND = 128

TOKEN_TILE = 512
SEQ_TILE = 512
COL_CHUNK = 512


def _params(*sem):
    return pltpu.CompilerParams(dimension_semantics=sem, vmem_limit_bytes=VMEM_LIMIT)


def _dot(a, b):
    return jnp.dot(a.astype(BF16), b.astype(BF16), preferred_element_type=F32)


def _dot_nt(a, b):
    return lax.dot_general(a.astype(BF16), b.astype(BF16), (((1,), (1,)), ((), ())),
                           preferred_element_type=F32)


def _dot_tn(a, b):
    return lax.dot_general(a.astype(BF16), b.astype(BF16), (((0,), (0,)), ((), ())),
                           preferred_element_type=F32)


def _split(x):
    hi = x.astype(BF16)
    return hi, (x - hi.astype(F32)).astype(BF16)


def _dot_exact_rhs(a, b01):
    hi, lo = _split(a)
    return (jnp.dot(hi, b01, preferred_element_type=F32)
            + jnp.dot(lo, b01, preferred_element_type=F32))


def _cumsum_rows(tri01, x):
    hi, lo = _split(x)
    return (jnp.dot(tri01, hi, preferred_element_type=F32)
            + jnp.dot(tri01, lo, preferred_element_type=F32))


def _dot_hilo(a, bh, bl):
    ah, al = _split(a)
    return (jnp.dot(ah, bh, preferred_element_type=F32)
            + jnp.dot(ah, bl, preferred_element_type=F32)
            + jnp.dot(al, bh, preferred_element_type=F32))


def _sigmoid(x):
    return 1.0 / (1.0 + jnp.exp(-x))


def _silu(x):
    return x * _sigmoid(x)


def _softplus(x):
    return jnp.maximum(x, 0.0) + jnp.log1p(jnp.exp(-jnp.abs(x)))


def _log_sigmoid(x):
    return -_softplus(-x)


def _tril01(n):
    r = lax.broadcasted_iota(jnp.int32, (n, n), 0)
    c = lax.broadcasted_iota(jnp.int32, (n, n), 1)
    return r >= c


def _norm_matmul_kernel(x_ref, g_ref, w_ref, *o_refs, segs):
    x = x_ref[...]
    h = x * lax.rsqrt(jnp.mean(x * x, axis=-1, keepdims=True) + EPS)
    h = (h * g_ref[...]).astype(BF16)
    c0 = 0
    for o_ref, width in zip(o_refs, segs):
        for j in range(0, width, COL_CHUNK):
            w = min(COL_CHUNK, width - j)
            o_ref[:, j:j + w] = jnp.dot(h, w_ref[:, c0 + j:c0 + j + w],
                                        preferred_element_type=F32).astype(o_ref.dtype)
        c0 += width


def _norm_matmul(x, g, w, segs, dtypes, tm=TOKEN_TILE):
    M, D = x.shape
    tm = min(tm, M)
    assert M % tm == 0 and w.shape == (D, sum(segs)) and all(s % LANE == 0 for s in segs)
    return pl.pallas_call(
        functools.partial(_norm_matmul_kernel, segs=tuple(segs)),
        grid=(M // tm,),
        in_specs=[pl.BlockSpec((tm, D), lambda i: (i, 0)),
                  pl.BlockSpec((1, D), lambda i: (0, 0)),
                  pl.BlockSpec(w.shape, lambda i: (0, 0))],
        out_specs=[pl.BlockSpec((tm, s), lambda i: (i, 0)) for s in segs],
        out_shape=[jax.ShapeDtypeStruct((M, s), dt) for s, dt in zip(segs, dtypes)],
        compiler_params=_params("parallel"),
        name="norm_matmul",
    )(x, g.reshape(1, D).astype(F32), w)


def _rms_rows(x_ref, g_ref):
    x = x_ref[...]
    h = x * lax.rsqrt(jnp.mean(x * x, axis=-1, keepdims=True) + EPS)
    return (h * g_ref[...]).astype(BF16)


def _col_chunks(width):
    return [(j, min(COL_CHUNK, width - j)) for j in range(0, width, COL_CHUNK)]


def _proj_store(h, w_ref, c0, o_ref, fn=None):
    width = o_ref.shape[-1]
    for j, w in _col_chunks(width):
        u = jnp.dot(h, w_ref[:, c0 + j:c0 + j + w], preferred_element_type=F32)
        if fn is not None:
            u = fn(u, slice(j, j + w))
        o_ref[:, j:j + w] = u.astype(o_ref.dtype)
    return c0 + width


def _gla_in_proj_kernel(x_ref, g_ref, w_ref, wgh_ref, wgl_ref, bg_ref,
                        q_ref, k_ref, v_ref, gate_ref, la_ref, xq_ref, *, qscale):
    h = _rms_rows(x_ref, g_ref)
    c = _proj_store(h, w_ref, 0, q_ref, lambda u, cs: u * qscale)
    c = _proj_store(h, w_ref, c, k_ref)
    c = _proj_store(h, w_ref, c, v_ref)
    c = _proj_store(h, w_ref, c, gate_ref, lambda u, cs: _silu(u))
    c = _proj_store(h, w_ref, c, xq_ref)
    glr = jnp.dot(h, w_ref[:, c:c + LANE], preferred_element_type=F32)
    for j, w in _col_chunks(la_ref.shape[-1]):
        cs = slice(j, j + w)
        pre = _dot_hilo(glr, wgh_ref[:, cs], wgl_ref[:, cs]) + bg_ref[:, cs]
        la_ref[:, cs] = _log_sigmoid(pre) * (1.0 / GLA_GATE_NORM)


def _hgrn_in_proj_kernel(x_ref, g_ref, w_ref, lb_ref, q_ref, k_ref, la_ref, v_ref, gate_ref, xq_ref):
    h = _rms_rows(x_ref, g_ref)
    c = _proj_store(h, w_ref, 0, q_ref, lambda u, cs: _silu(u))
    for j, w in _col_chunks(k_ref.shape[-1]):
        cs = slice(j, j + w)
        lb = lb_ref[:, cs]
        f = jnp.dot(h, w_ref[:, c + j:c + j + w], preferred_element_type=F32)
        fg = lb + (1.0 - lb) * _sigmoid(f)
        k_ref[:, cs] = 1.0 - fg
        la_ref[:, cs] = jnp.log(fg)
    c += k_ref.shape[-1]
    c = _proj_store(h, w_ref, c, v_ref)
    c = _proj_store(h, w_ref, c, gate_ref, lambda u, cs: _sigmoid(u))
    _proj_store(h, w_ref, c, xq_ref)


def _ssd_in_proj_kernel(x_ref, g_ref, w_ref, cw_ref, cb_ref, dtb_ref,
                        gate_ref, xc_ref, dt_ref, xq_ref, carry_ref):
    @pl.when(pl.program_id(1) == 0)
    def _():
        carry_ref[...] = jnp.zeros_like(carry_ref)

    T = x_ref.shape[0]
    h = _rms_rows(x_ref, g_ref)
    c = _proj_store(h, w_ref, 0, gate_ref, lambda u, cs: _silu(u))
    for j in range(0, xc_ref.shape[-1], 2 * LANE):
        cs = slice(j, j + 2 * LANE)
        u = jnp.dot(h, w_ref[:, c + j:c + j + 2 * LANE], preferred_element_type=F32)
        prev = carry_ref[:, cs]
        carry_ref[:, cs] = u[T - SUBLANE:T, :]
        xc_ref[:, cs] = _silu(_causal_taps(u, prev, cw_ref[:, cs], cb_ref[:, cs]))
    c += xc_ref.shape[-1]
    c = _proj_store(h, w_ref, c, dt_ref, lambda u, cs: _softplus(u + dtb_ref[:, cs]))
    _proj_store(h, w_ref, c, xq_ref)


def _in_proj_call(kern, x, g, w, consts, widths, dtypes, name, sequential=False, scratch=()):
    B, S, D = x.shape
    T = min(TOKEN_TILE, S)
    assert S % T == 0 and all(wd % LANE == 0 for wd in widths)
    return pl.pallas_call(
        kern,
        grid=(B, S // T),
        in_specs=[_seq_spec(T, D), _const_spec((1, D)),
                  pl.BlockSpec(w.shape, lambda b, t: (0, 0), pipeline_mode=pl.Buffered(1))]
                 + [_const_spec(a.shape) for a in consts],
        out_specs=[_seq_spec(T, wd) for wd in widths],
        out_shape=[jax.ShapeDtypeStruct((B, S, wd), dt) for wd, dt in zip(widths, dtypes)],
        scratch_shapes=list(scratch),
        compiler_params=_params("parallel", "arbitrary" if sequential else "parallel"),
        name=name,
    )(x, g.reshape(1, D).astype(F32), w, *consts)


def _head_rms_gate(o, g, gate, dv):
    y = o * lax.rsqrt(jnp.sum(o * o, axis=-1, keepdims=True) * (1.0 / dv) + EPS)
    return (y * g) * gate


GLA_GROUP = 4


def _gla_group(q_ref, k_ref, v_ref, gate_ref, la_ref, on, o_ref, st_ref, r0, NC, H, KP, VP, dv,
               tri01, tril):
    C = CHUNK
    rows = [slice(r0 + i * C, r0 + (i + 1) * C) for i in range(NC)]
    idx = [(i, h) for i in range(NC) for h in range(H)]
    bs = [_cumsum_rows(tri01, la_ref[r, :]) for r in rows]
    att, qi, ksb, dec = {}, {}, {}, {}
    for i in range(NC):
        q, k, b = q_ref[rows[i], :], k_ref[rows[i], :], bs[i]
        for h in range(H):
            ks = slice(h * KP, (h + 1) * KP)
            bh, bl, br = b[:, ks], b[C - 1:C, ks], b[C // 2 - 1:C // 2, ks]
            qe = q[:, ks] * jnp.exp(bh - br)
            ke = k[:, ks] * jnp.exp(br - bh)
            att[i, h] = _dot_nt(qe, ke)
            qi[i, h] = (qe * jnp.exp(br)).astype(BF16)
            ksb[i, h] = (ke * jnp.exp(bl - br)).astype(BF16)
            dec[i, h] = jnp.exp(bl)
    o_intra, upd = {}, {}
    for i, h in idx:
        vh = v_ref[rows[i], h * VP:(h + 1) * VP]
        o_intra[i, h] = _dot(jnp.where(tril, att[i, h], 0.0), vh)
        upd[i, h] = _dot_tn(vh, ksb[i, h])
    o_inter = {}
    for h in range(H):
        st = st_ref[h]
        for i in range(NC):
            o_inter[i, h] = _dot_nt(qi[i, h], st)
            st = dec[i, h] * st + upd[i, h]
        st_ref[h] = st
    for i, h in idx:
        vs = slice(h * VP, (h + 1) * VP)
        o = o_intra[i, h] + o_inter[i, h]
        o_ref[rows[i], vs] = _head_rms_gate(o, on, gate_ref[rows[i], vs], dv).astype(o_ref.dtype)


def _gla_kernel(q_ref, k_ref, v_ref, gate_ref, la_ref, on_ref, o_ref, st_ref, *, H, KP, VP, dv, T):
    @pl.when(pl.program_id(1) == 0)
    def _():
        st_ref[...] = jnp.zeros_like(st_ref)

    tril = _tril01(CHUNK)
    tri01 = tril.astype(BF16)
    on = on_ref[...]
    NC = GLA_GROUP
    for r0 in range(0, T, NC * CHUNK):
        _gla_group(q_ref, k_ref, v_ref, gate_ref, la_ref, on, o_ref, st_ref, r0, NC, H, KP, VP, dv,
                   tri01, tril)


def _seq_spec(T, width, col=0):
    return pl.BlockSpec((None, T, width), lambda b, t: (b, t, col))


def _const_spec(shape):
    return pl.BlockSpec(shape, lambda b, t: (0,) * len(shape))


def _gla_mixer(q, k, v, gate, la, on, *, H, KP, VP, dv, name):
    B, S, _ = q.shape
    T = min(SEQ_TILE, S)
    assert S % T == 0 and T % CHUNK == 0
    return pl.pallas_call(
        functools.partial(_gla_kernel, H=H, KP=KP, VP=VP, dv=dv, T=T),
        grid=(B, S // T),
        in_specs=[_seq_spec(T, H * KP), _seq_spec(T, H * KP), _seq_spec(T, H * VP),
                  _seq_spec(T, H * VP), _seq_spec(T, H * KP), _const_spec(on.shape)],
        out_specs=_seq_spec(T, H * VP),
        out_shape=jax.ShapeDtypeStruct((B, S, H * VP), BF16),
        scratch_shapes=[pltpu.VMEM((H, VP, KP), F32)],
        compiler_params=_params("parallel", "arbitrary"),
        name=name,
    )(q, k, v, gate, la, on)


def _ssd_kernel(gate_ref, xc_ref, dt_ref, alog_ref, dsk_ref, ng_ref, e_ref, o_ref, st_ref,
                *, T, DM, N, G):
    @pl.when(pl.program_id(1) == 0)
    def _():
        st_ref[...] = jnp.zeros_like(st_ref)

    C = CHUNK
    GW = DM // G
    tril = _tril01(C)
    tri01 = tril.astype(BF16)
    e01 = e_ref[...]
    a_neg = -jnp.exp(alog_ref[...])
    dsk = dsk_ref[...]
    ng = ng_ref[...]
    P = SSM_HEAD_DIM
    rr = lax.broadcasted_iota(jnp.int32, (C, DM), 0)
    cc = lax.broadcasted_iota(jnp.int32, (C, DM), 1)
    diag_sel = (cc & (P - 1)) == rr
    r2 = lax.broadcasted_iota(jnp.int32, (C, 2 * P), 0)
    c2 = lax.broadcasted_iota(jnp.int32, (C, 2 * P), 1)
    causal2 = r2 >= (c2 & (P - 1))
    lo_half = lax.broadcasted_iota(jnp.int32, (C, 2 * P), 1) < P

    def group(r0, NC):
        rows = [slice(r0 + i * C, r0 + (i + 1) * C) for i in range(NC)]
        xs, xdt, acs_e, acs_row, acs_last, bm, cm, cb2 = {}, {}, {}, {}, {}, {}, {}, {}
        for i in range(NC):
            xs[i] = xc_ref[rows[i], 0:DM]
            dt = dt_ref[rows[i], :]
            acs = _cumsum_rows(tri01, dt * a_neg)
            dt_e = _dot_exact_rhs(dt, e01)
            acs_e[i] = _dot_exact_rhs(acs, e01)
            acs_row[i] = jnp.sum(jnp.where(diag_sel, acs_e[i], 0.0), axis=0, keepdims=True)
            acs_last[i] = acs_e[i][C - 1:C, :]
            xdt[i] = xs[i] * dt_e
            for g in range(G):
                bm[i, g] = xc_ref[rows[i], DM + g * N:DM + (g + 1) * N]
                cm[i, g] = xc_ref[rows[i], DM + (G + g) * N:DM + (G + g + 1) * N]
                cb2[i, g] = _dot_nt(cm[i, g], jnp.concatenate([bm[i, g], bm[i, g]], axis=0))
        y_diag, upd = {}, {}
        for i in range(NC):
            for g in range(G):
                gs = slice(g * GW, (g + 1) * GW)
                yd = []
                for p in range(GW // (2 * P)):
                    ps = slice(g * GW + p * 2 * P, g * GW + (p + 1) * 2 * P)
                    seg = acs_e[i][:, ps] - acs_row[i][:, ps]
                    lmat = jnp.exp(jnp.where(causal2, seg, -jnp.inf))
                    xp = xdt[i][:, ps]
                    rhs = jnp.concatenate([jnp.where(lo_half, xp, 0.0),
                                           jnp.where(lo_half, 0.0, xp)], axis=0)
                    yd.append(_dot(cb2[i, g] * lmat, rhs))
                y_diag[i, g] = jnp.concatenate(yd, axis=1)
                xe = xdt[i][:, gs] * jnp.exp(acs_last[i][:, gs] - acs_e[i][:, gs])
                upd[i, g] = _dot_tn(bm[i, g], xe)
        y_off = {}
        for g in range(G):
            gs = slice(g * GW, (g + 1) * GW)
            st = st_ref[g]
            for i in range(NC):
                y_off[i, g] = _dot(cm[i, g], st)
                st = jnp.exp(acs_last[i][:, gs]) * st + upd[i, g]
            st_ref[g] = st
        for i in range(NC):
            for g in range(G):
                gs = slice(g * GW, (g + 1) * GW)
                y = y_diag[i, g] + y_off[i, g] * jnp.exp(acs_e[i][:, gs])
                y = y + dsk[:, gs] * xs[i][:, gs]
                y = y * gate_ref[rows[i], gs]
                y = y * lax.rsqrt(jnp.mean(y * y, axis=-1, keepdims=True) + EPS)
                o_ref[rows[i], gs] = (y * ng[:, gs]).astype(o_ref.dtype)

    for r0 in range(0, T, GLA_GROUP * C):
        group(r0, GLA_GROUP)


def _ssd_mixer(gate, xc, dt, alog, dsk, ng, e01):
    B, S, DM = gate.shape
    W = xc.shape[-1]
    T = min(SEQ_TILE, S)
    assert S % T == 0 and T % CHUNK == 0
    G, N = SSM_GROUPS, SSM_STATE
    return pl.pallas_call(
        functools.partial(_ssd_kernel, T=T, DM=DM, N=N, G=G),
        grid=(B, S // T),
        in_specs=[_seq_spec(T, DM), _seq_spec(T, W), _seq_spec(T, LANE),
                  _const_spec(alog.shape), _const_spec(dsk.shape), _const_spec(ng.shape),
                  _const_spec(e01.shape)],
        out_specs=_seq_spec(T, DM),
        out_shape=jax.ShapeDtypeStruct((B, S, DM), BF16),
        scratch_shapes=[pltpu.VMEM((G, N, DM // G), F32)],
        compiler_params=_params("parallel", "arbitrary"),
        name="ssd_mixer",
    )(gate, xc, dt, alog, dsk, ng, e01)


def _dil_in_proj_kernel(x_ref, g_ref, w_ref, pos_ref, qn_ref, kn_ref, invf_ref, sign_ref, *refs,
                        dils, NH, HD):
    NG = len(dils)
    GW = NH * HD
    outs, xq_ref, buf_ref = refs[:3 * NG], refs[3 * NG], refs[3 * NG + 1]
    tm = x_ref.shape[0]
    x = x_ref[...]
    h = x * lax.rsqrt(jnp.mean(x * x, axis=-1, keepdims=True) + EPS)
    h = (h * g_ref[...]).astype(BF16)
    ang = pos_ref[...].astype(F32) * invf_ref[...]
    cos = jnp.cos(ang)
    sin = jnp.sin(ang) * sign_ref[...]
    gains = (qn_ref[...], kn_ref[...])
    for kind in range(3):
        for g, r in enumerate(dils):
            c0 = (kind * NG + g) * GW
            u = jnp.dot(h, w_ref[:, c0:c0 + GW], preferred_element_type=F32)
            o_ref = outs[kind * NG + g]
            for hh in range(NH):
                hs = slice(hh * HD, (hh + 1) * HD)
                y = u[:, hs]
                if kind < 2:
                    y = (y * lax.rsqrt(jnp.mean(y * y, axis=-1, keepdims=True) + EPS)) * gains[kind]
                    y = y * cos + pltpu.roll(y, HD // 2, 1) * sin
                if r == 1:
                    o_ref[0, :, hs] = y.astype(o_ref.dtype)
                else:
                    slab = ((kind * NG + g) % 2) * NH + hh
                    buf_ref[slab] = y
                    for rho in range(r):
                        o_ref[rho, :, hs] = buf_ref[slab, pl.ds(rho, tm // r, stride=r), :].astype(
                            o_ref.dtype)
    xq_ref[...] = jnp.dot(h, w_ref[:, 3 * NG * GW:], preferred_element_type=F32)


def _dil_in_proj(x, g, w, pos, qn, kn, dils):
    B, S, D = x.shape
    NH, HD = DIL_HEADS, DIL_HEAD_DIM
    GW = NH * HD
    NG = len(dils)
    T = min(TOKEN_TILE, S)
    WQ = w.shape[1] - 3 * NG * GW
    assert S % T == 0 and all(T % (r * 2 * SUBLANE) == 0 for r in dils)
    half = HD // 2
    inv_freq = ROPE_THETA ** (-jnp.arange(half, dtype=F32) / half)
    invf = jnp.concatenate([inv_freq, inv_freq]).reshape(1, HD)
    sign = jnp.concatenate([-jnp.ones((half,), F32), jnp.ones((half,), F32)]).reshape(1, HD)
    grp_spec = [pl.BlockSpec((None, r, T // r, GW), lambda b, t: (b, 0, t, 0)) for r in dils]
    grp_shape = [jax.ShapeDtypeStruct((B, r, S // r, GW), BF16) for r in dils]
    res = pl.pallas_call(
        functools.partial(_dil_in_proj_kernel, dils=tuple(dils), NH=NH, HD=HD),
        grid=(B, S // T),
        in_specs=[_seq_spec(T, D), _const_spec((1, D)),
                  pl.BlockSpec(w.shape, lambda b, t: (0, 0), pipeline_mode=pl.Buffered(1)),
                  _seq_spec(T, 1), _const_spec((1, HD)), _const_spec((1, HD)),
                  _const_spec((1, HD)), _const_spec((1, HD))],
        out_specs=grp_spec * 3 + [_seq_spec(T, WQ)],
        out_shape=grp_shape * 3 + [jax.ShapeDtypeStruct((B, S, WQ), F32)],
        scratch_shapes=[pltpu.VMEM((2 * NH, T, HD), F32)],
        compiler_params=_params("parallel", "parallel"),
        name="dilated_in_proj",
    )(x, g.reshape(1, D).astype(F32), w, pos, qn.reshape(1, HD), kn.reshape(1, HD), invf, sign)
    return res[:NG], res[NG:2 * NG], res[2 * NG:3 * NG], res[3 * NG]


def _dil_attn_kernel(q_ref, kp_ref, kc_ref, vp_ref, vc_ref, o_ref, lse_ref, *, nq, W, NH, HD):
    n0 = pl.program_id(2)
    Q = DIL_BLOCK
    i_idx = lax.broadcasted_iota(jnp.int32, (Q, 2 * Q), 0)
    j_idx = lax.broadcasted_iota(jnp.int32, (Q, 2 * Q), 1)
    dist = Q + i_idx - j_idx
    band = (dist >= 0) & (dist <= W)
    jmin = jnp.where(n0 > 0, 0, Q)
    band_first = band & (j_idx >= jmin)
    lane = lax.broadcasted_iota(jnp.int32, (Q, LANE), 1)
    scale = HD ** -0.5
    idx = [(i, h) for i in range(nq) for h in range(NH)]
    hsl = lambda h: slice(h * HD, (h + 1) * HD)
    s = {}
    for i, h in idx:
        if i == 0:
            kk = jnp.concatenate([kp_ref[:, hsl(h)], kc_ref[0:Q, hsl(h)]], axis=0)
        else:
            kk = kc_ref[(i - 1) * Q:(i + 1) * Q, hsl(h)]
        s[i, h] = _dot_nt(q_ref[i * Q:(i + 1) * Q, hsl(h)], kk)
    p, m, l = {}, {}, {}
    for i, h in idx:
        sm = jnp.where(band_first if i == 0 else band, s[i, h] * scale, -jnp.inf)
        m[i, h] = jnp.max(sm, axis=-1, keepdims=True)
        p[i, h] = jnp.exp(sm - m[i, h])
        l[i, h] = jnp.sum(p[i, h], axis=-1, keepdims=True)
    for i, h in idx:
        if i == 0:
            vv = jnp.concatenate([vp_ref[:, hsl(h)], vc_ref[0:Q, hsl(h)]], axis=0)
        else:
            vv = vc_ref[(i - 1) * Q:(i + 1) * Q, hsl(h)]
        o_ref[i * Q:(i + 1) * Q, hsl(h)] = _dot(p[i, h], vv) * (1.0 / l[i, h])
    for i in range(nq):
        lse_tile = jnp.zeros((Q, LANE), F32)
        for h in range(NH):
            lse_tile = jnp.where(lane == h, m[i, h] + jnp.log(l[i, h]), lse_tile)
        lse_ref[i * Q:(i + 1) * Q, :] = lse_tile


def _dil_attention(q, k, v, window, dilation):
    B, r, L, GW = q.shape
    NH, HD, Q = DIL_HEADS, DIL_HEAD_DIM, DIL_BLOCK
    W = window // dilation
    assert r == dilation and L % Q == 0 and W <= Q
    nq = min(4, L // Q)
    assert (L // Q) % nq == 0
    cur = pl.BlockSpec((None, None, nq * Q, GW), lambda b, rho, n: (b, rho, n, 0))
    prev = pl.BlockSpec((None, None, Q, GW),
                        lambda b, rho, n: (b, rho, jnp.maximum(n * nq - 1, 0), 0))
    return pl.pallas_call(
        functools.partial(_dil_attn_kernel, nq=nq, W=W, NH=NH, HD=HD),
        grid=(B, r, L // (nq * Q)),
        in_specs=[cur, prev, cur, prev, cur],
        out_specs=[cur, pl.BlockSpec((None, None, nq * Q, LANE), lambda b, rho, n: (b, rho, n, 0))],
        out_shape=[jax.ShapeDtypeStruct((B, r, L, GW), F32),
                   jax.ShapeDtypeStruct((B, r, L, LANE), F32)],
        compiler_params=_params("parallel", "parallel", "arbitrary"),
        name=f"dilated_attention_r{r}",
    )(q, k, k, v, v)


def _dil_merge_kernel(*refs, dils, NH, HD):
    NG = len(dils)
    o_refs, l_refs, out_ref, obuf_ref, lbuf_ref = (refs[:NG], refs[NG:2 * NG], refs[2 * NG],
                                                   refs[2 * NG + 1], refs[2 * NG + 2])
    tm = out_ref.shape[0]
    ls = []
    for g, r in enumerate(dils):
        if r == 1:
            ls.append(l_refs[g][0])
            continue
        for rho in range(r):
            rows = pl.ds(rho, tm // r, stride=r)
            lbuf_ref[g, rows, :] = l_refs[g][rho]
            for h in range(NH):
                obuf_ref[g * NH + h, rows, :] = o_refs[g][rho, :, h * HD:(h + 1) * HD]
        ls.append(lbuf_ref[g])
    m = functools.reduce(jnp.maximum, ls)
    es = [jnp.exp(l - m) for l in ls]
    inv = 1.0 / functools.reduce(lambda a, b: a + b, es)
    for h in range(NH):
        hs = slice(h * HD, (h + 1) * HD)
        acc = None
        for g, r in enumerate(dils):
            o = o_refs[g][0, :, hs] if r == 1 else obuf_ref[g * NH + h]
            term = (es[g][:, h:h + 1] * inv[:, h:h + 1]) * o
            acc = term if acc is None else acc + term
        out_ref[:, hs] = acc.astype(out_ref.dtype)


def _dil_merge(outs, lses, dils):
    B, _, L0, GW = outs[0].shape
    S = L0 * dils[0]
    NH, HD = DIL_HEADS, DIL_HEAD_DIM
    NG = len(dils)
    T = min(TOKEN_TILE, S)
    grp = lambda w: [pl.BlockSpec((None, r, T // r, w), lambda b, t: (b, 0, t, 0)) for r in dils]
    return pl.pallas_call(
        functools.partial(_dil_merge_kernel, dils=tuple(dils), NH=NH, HD=HD),
        grid=(B, S // T),
        in_specs=grp(GW) + grp(LANE),
        out_specs=_seq_spec(T, GW),
        out_shape=jax.ShapeDtypeStruct((B, S, GW), BF16),
        scratch_shapes=[pltpu.VMEM((NG * NH, T, HD), F32), pltpu.VMEM((NG, T, LANE), F32)],
        compiler_params=_params("parallel", "parallel"),
        name="dilated_merge",
    )(*outs, *lses)


def _pair_rms(x, lo, hd):
    sq = x * x
    s_lo = jnp.sum(jnp.where(lo, sq, 0.0), axis=-1, keepdims=True)
    s_hi = jnp.sum(jnp.where(lo, 0.0, sq), axis=-1, keepdims=True)
    r = jnp.where(lo, lax.rsqrt(s_lo * (1.0 / hd) + EPS), lax.rsqrt(s_hi * (1.0 / hd) + EPS))
    return x * r


def _memory_xattn(xq_ref, kv_ref, qn2, kn2, hd):
    T, W = xq_ref.shape
    M = kv_ref.shape[0]
    assert 2 * hd == LANE and W % LANE == 0
    lo_q = lax.broadcasted_iota(jnp.int32, (T, LANE), 1) < hd
    lo_k = lax.broadcasted_iota(jnp.int32, (M, LANE), 1) < hd
    scale = hd ** -0.5
    outs = []
    for p in range(W // LANE):
        ps = slice(p * LANE, (p + 1) * LANE)
        q = _pair_rms(xq_ref[:, ps], lo_q, hd) * qn2
        k = (_pair_rms(kv_ref[:, ps], lo_k, hd) * kn2).astype(BF16)
        v = kv_ref[:, W + p * LANE:W + (p + 1) * LANE]
        o = None
        for first in (True, False):
            qm = jnp.where(lo_q, q, 0.0) if first else jnp.where(lo_q, 0.0, q)
            vm = jnp.where(lo_k, v, 0.0) if first else jnp.where(lo_k, 0.0, v)
            s = _dot_nt(qm, k) * scale
            m = jnp.max(s, axis=-1, keepdims=True)
            e = jnp.exp(s - m)
            l = jnp.sum(e, axis=-1, keepdims=True)
            t = _dot(e, vm) * (1.0 / l)
            o = t if o is None else o + t
        outs.append(o)
    return jnp.concatenate(outs, axis=1).astype(BF16)


def _causal_taps(u, prev, cw, cb):
    K = cw.shape[0]
    sub = lax.broadcasted_iota(jnp.int32, (SUBLANE, u.shape[1]), 0)
    acc = cb + cw[K - 1:K, :] * u
    for k in range(1, K):
        rolled = pltpu.roll(u, k, 0)
        head = jnp.where(sub < k, pltpu.roll(prev, k, 0), rolled[0:SUBLANE, :])
        shifted = jnp.concatenate([head, rolled[SUBLANE:, :]], axis=0)
        acc = acc + cw[K - 1 - k:K - k, :] * shifted
    return acc


def _tail_kernel(x_ref, tok_ref, xq_ref, kv_ref, qn_ref, kn_ref, wt_ref, wx_ref, g_ref, wu_ref,
                 cw_ref, cb_ref, wd_ref, o_ref, carry_ref, act_ref, x1_ref, *, T, FF, FC):
    @pl.when(pl.program_id(1) == 0)
    def _():
        carry_ref[...] = jnp.zeros_like(carry_ref)

    D = o_ref.shape[1]
    xa = _memory_xattn(xq_ref, kv_ref, qn_ref[...], kn_ref[...], XA_HEAD_DIM)
    tok = tok_ref[...]
    for n in range(0, D, COL_CHUNK):
        cs = slice(n, n + COL_CHUNK)
        x1_ref[:, cs] = (x_ref[:, cs] + jnp.dot(tok, wt_ref[:, cs], preferred_element_type=F32)
                         + jnp.dot(xa, wx_ref[:, cs], preferred_element_type=F32))

    x = x1_ref[...]
    h = x * lax.rsqrt(jnp.mean(x * x, axis=-1, keepdims=True) + EPS)
    h = (h * g_ref[...]).astype(BF16)

    def conv(cols):
        u = jnp.dot(h, wu_ref[:, cols], preferred_element_type=F32)
        prev = carry_ref[:, cols]
        carry_ref[:, cols] = u[T - SUBLANE:T, :]
        return _causal_taps(u, prev, cw_ref[:, cols], cb_ref[:, cols])

    for j in range(0, FF, FC):
        gate = conv(slice(j, j + FC))
        val = conv(slice(FF + j, FF + j + FC))
        act_ref[:, j:j + FC] = (_silu(gate) * val).astype(BF16)
    for n in range(0, D, COL_CHUNK):
        cs = slice(n, n + COL_CHUNK)
        o_ref[:, cs] = x1_ref[:, cs] + jnp.dot(act_ref[...], wd_ref[:, cs],
                                               preferred_element_type=F32)


def _layer_tail(x, tok, xq, kv_all, layer, qn2, kn2, wt, wx, g, wu, cw, cb, wd):
    B, S, D = x.shape
    FF = wd.shape[0]
    WQ = xq.shape[-1]
    Mm = kv_all.shape[1]
    T = min(SEQ_TILE, S)
    FC = 256
    assert FF % FC == 0 and S % T == 0 and D % COL_CHUNK == 0
    resident = lambda a: pl.BlockSpec(a.shape, lambda b, t: (0, 0), pipeline_mode=pl.Buffered(1))
    return pl.pallas_call(
        functools.partial(_tail_kernel, T=T, FF=FF, FC=FC),
        grid=(B, S // T),
        in_specs=[_seq_spec(T, D), _seq_spec(T, tok.shape[-1]), _seq_spec(T, WQ),
                  pl.BlockSpec((None, Mm, 2 * WQ), lambda b, t: (b, 0, layer)),
                  _const_spec(qn2.shape), _const_spec(kn2.shape), resident(wt), resident(wx),
                  _const_spec((1, D)), resident(wu), _const_spec(cw.shape), _const_spec(cb.shape),
                  resident(wd)],
        out_specs=_seq_spec(T, D),
        out_shape=jax.ShapeDtypeStruct((B, S, D), F32),
        scratch_shapes=[pltpu.VMEM((SUBLANE, 2 * FF), F32), pltpu.VMEM((T, FF), BF16),
                        pltpu.VMEM((T, D), F32)],
        compiler_params=_params("parallel", "arbitrary"),
        name="layer_tail",
    )(x, tok, xq, kv_all, qn2, kn2, wt, wx, g.reshape(1, D), wu, cw, cb, wd)


def _pad_heads_cols(w, H, d, dp):
    R = w.shape[0]
    return jnp.pad(w.reshape(R, H, d), ((0, 0), (0, 0), (0, dp - d))).reshape(R, H * dp)


def _pad_heads_rows(w, H, d, dp):
    C = w.shape[1]
    return jnp.pad(w.reshape(H, d, C), ((0, 0), (0, dp - d), (0, 0))).reshape(H * dp, C)


def _pad_cols(w, n):
    return jnp.pad(w, ((0, 0), (0, n - w.shape[1])))


def _cols(w, sizes):
    out, c = [], 0
    for s in sizes:
        out.append(w[:, c:c + s])
        c += s
    return out


def _out_weights(w_out, d_tok, tok_pad=None):
    wt, wx = w_out[:d_tok], w_out[d_tok:]
    if tok_pad is not None:
        wt = _pad_heads_rows(wt, *tok_pad)
    return wt.astype(BF16), wx.astype(BF16)


def _gla_layer(xf, norm_g, B, S, w_in, w_gate2, b_gate, o_norm):
    D = xf.shape[1]
    d_mix = 3 * D // 4
    H = GLA_HEADS
    dk, dv = d_mix // 2 // H, d_mix // H
    KP, VP = -(-dk // LANE) * LANE, -(-dv // LANE) * LANE
    q, k, v, glr, og, xq = _cols(w_in, [H * dk, H * dk, d_mix, GLA_RANK, d_mix, XA_HEADS * XA_HEAD_DIM])
    w = jnp.concatenate([_pad_heads_cols(q, H, dk, KP), _pad_heads_cols(k, H, dk, KP),
                         _pad_heads_cols(v, H, dv, VP), _pad_heads_cols(og, H, dv, VP),
                         xq, _pad_cols(glr, LANE)], axis=1).astype(BF16)
    wg = jnp.pad(_pad_heads_cols(w_gate2, H, dk, KP), ((0, LANE - GLA_RANK), (0, 0)))
    wgh, wgl = _split(wg)
    bg = _pad_heads_cols(b_gate.reshape(1, -1), H, dk, KP)
    on = _pad_cols(o_norm.reshape(1, -1), VP)
    q, k, v, gate, la, xq = _in_proj_call(
        functools.partial(_gla_in_proj_kernel, qscale=dk ** -0.5), xf.reshape(B, S, D), norm_g, w,
        [wgh, wgl, bg], [H * KP, H * KP, H * VP, H * VP, H * KP, XA_HEADS * XA_HEAD_DIM],
        [F32, F32, BF16, F32, F32, F32], "gla_in_proj")
    tok = _gla_mixer(q, k, v, gate, la, on, H=H, KP=KP, VP=VP, dv=dv, name="gla_mixer")
    return tok, xq, (H, dv, VP)


def _dilated_layer(xf, norm_g, B, S, w_in, q_norm, k_norm, positions):
    NG = len(DIL_GROUPS)
    GW = DIL_HEADS * DIL_HEAD_DIM
    dils = [d for _, d in DIL_GROUPS]
    q, k, v, xq = _cols(w_in, [NG * GW] * 3 + [XA_HEADS * XA_HEAD_DIM])
    w = jnp.concatenate([q, k, v, xq], axis=1).astype(BF16)
    qs, ks, vs, xq = _dil_in_proj(xf.reshape(B, S, -1), norm_g, w, positions.reshape(B, S, 1),
                                  q_norm, k_norm, dils)
    outs, lses = [], []
    for g, (window, dilation) in enumerate(DIL_GROUPS):
        o, lse = _dil_attention(qs[g], ks[g], vs[g], window, dilation)
        outs.append(o)
        lses.append(lse)
    tok = _dil_merge(outs, lses, dils)
    return tok, xq.reshape(B * S, -1), None


def _mamba_layer(xf, norm_g, B, S, w_in, conv_w, conv_b, dt_bias, a_log, d_skip, norm_gain):
    D = xf.shape[1]
    d_mix = 3 * D // 4
    GN = SSM_GROUPS * SSM_STATE
    NHD = d_mix // SSM_HEAD_DIM
    z, xbc, dt, xq = _cols(w_in, [d_mix, d_mix + 2 * GN, NHD, XA_HEADS * XA_HEAD_DIM])
    w = jnp.concatenate([z, xbc, _pad_cols(dt, LANE), xq], axis=1).astype(BF16)
    padl = lambda a: _pad_cols(a.reshape(1, -1).astype(F32), LANE)
    W = d_mix + 2 * GN
    assert W % (2 * LANE) == 0
    gate, xc, dt, xq = _in_proj_call(
        _ssd_in_proj_kernel, xf.reshape(B, S, D), norm_g, w,
        [conv_w, conv_b.reshape(1, -1), padl(dt_bias)], [d_mix, W, LANE, XA_HEADS * XA_HEAD_DIM],
        [F32, F32, F32, F32], "ssd_in_proj", sequential=True,
        scratch=[pltpu.VMEM((SUBLANE, W), F32)])
    head = lax.broadcasted_iota(jnp.int32, (LANE, d_mix), 0)
    col = lax.broadcasted_iota(jnp.int32, (LANE, d_mix), 1)
    e01 = (col // SSM_HEAD_DIM == head).astype(BF16)
    tok = _ssd_mixer(gate, xc, dt, padl(a_log), jnp.repeat(d_skip, SSM_HEAD_DIM).reshape(1, -1),
                     norm_gain.reshape(1, -1), e01)
    return tok, xq, None


def _hgrn_layer(xf, norm_g, B, S, w_in, lower_bounds, o_norm, layer):
    D = xf.shape[1]
    d_mix = 3 * D // 4
    H = d_mix // HGRN_EXPAND
    KP, VP = HGRN_EXPAND, d_mix // H
    q, f, i, og, xq = _cols(w_in, [H * KP, H * KP, d_mix, d_mix, XA_HEADS * XA_HEAD_DIM])
    w = jnp.concatenate([q, f, i, og, xq], axis=1).astype(BF16)
    lbs = jnp.cumsum(jax.nn.softmax(lower_bounds.astype(F32), axis=0), axis=0)
    lb = (lbs[layer] - lbs[0]).reshape(1, -1)
    q, k, la, v, gate, xq = _in_proj_call(
        _hgrn_in_proj_kernel, xf.reshape(B, S, D), norm_g, w, [lb],
        [H * KP, H * KP, H * KP, d_mix, d_mix, XA_HEADS * XA_HEAD_DIM],
        [F32, F32, F32, BF16, F32, F32], "hgrn_in_proj")
    tok = _gla_mixer(q, k, v, gate, la, o_norm.reshape(1, -1), H=H, KP=KP, VP=VP, dv=VP,
                     name="hgrn_mixer")
    return tok, xq, None


def kernel(x, mem, positions, mem_norm, mix_norm, xa_w_kv, xa_q_norm, xa_k_norm, ffn_norm, ffn_w_up, ffn_conv_w, ffn_conv_b, ffn_w_down, a_w_in, a_w_gate2, a_b_gate, a_o_norm, a_w_out, b_w_in, b_q_norm, b_k_norm, b_w_out, c_w_in, c_conv_w, c_conv_b, c_dt_bias, c_a_log, c_d, c_norm, c_w_out, d_w_in, d_lower_bounds, d_o_norm, d_w_out):
    B, S, D = x.shape
    depth = mix_norm.shape[0]
    Mm = mem.shape[1]
    wkv = jnp.concatenate([xa_w_kv[i] for i in range(depth)], axis=1).astype(BF16)
    (kv_all,) = _norm_matmul(mem.reshape(B * Mm, D), mem_norm, wkv, [wkv.shape[1]], [F32])
    kv_all = kv_all.reshape(B, Mm, -1)
    pair = lambda g: jnp.concatenate([g, g]).reshape(1, -1)

    xf = x.reshape(B * S, D)
    for i in range(depth):
        kind = i % N_MIXERS
        if kind == 0:
            tok, xq, tok_pad = _gla_layer(xf, mix_norm[i], B, S, a_w_in, a_w_gate2, a_b_gate, a_o_norm)
            w_out, d_tok = a_w_out, 3 * D // 4
        elif kind == 1:
            tok, xq, tok_pad = _dilated_layer(xf, mix_norm[i], B, S, b_w_in, b_q_norm, b_k_norm, positions)
            w_out, d_tok = b_w_out, DIL_HEADS * DIL_HEAD_DIM
        elif kind == 2:
            tok, xq, tok_pad = _mamba_layer(xf, mix_norm[i], B, S, c_w_in, c_conv_w, c_conv_b,
                                            c_dt_bias, c_a_log, c_d, c_norm)
            w_out, d_tok = c_w_out, 3 * D // 4
        else:
            tok, xq, tok_pad = _hgrn_layer(xf, mix_norm[i], B, S, d_w_in, d_lower_bounds, d_o_norm, i)
            w_out, d_tok = d_w_out, 3 * D // 4
        wt, wx = _out_weights(w_out, d_tok, tok_pad)
        xf = _layer_tail(xf.reshape(B, S, D), tok.reshape(B, S, -1), xq.reshape(B, S, -1), kv_all, i,
                         pair(xa_q_norm[i]), pair(xa_k_norm[i]), wt, wx, ffn_norm[i],
                         ffn_w_up[i].astype(BF16), ffn_conv_w[i], ffn_conv_b[i].reshape(1, -1),
                         ffn_w_down[i].astype(BF16)).reshape(B * S, D)
    return xf.reshape(B, S, D)
```

```python
import functools
import math

import jax
import jax.numpy as jnp
from jax import lax
from jax.experimental import pallas as pl
from jax.experimental.pallas import tpu as pltpu

F32 = jnp.float32
BF16 = jnp.bfloat16

LANE = 128
SUBLANE = 8
VMEM_LIMIT = 56 * 1024 * 1024

EPS = 1e-6
ROPE_THETA = 10000.0
CHUNK = 64
N_MIXERS = 4
XA_HEADS, XA_HEAD_DIM = 4, 64
GLA_HEADS, GLA_RANK, GLA_GATE_NORM = 4, 16, 16.0
DIL_GROUPS = ((128, 1), (512, 4), (2048, 16))
DIL_HEADS, DIL_HEAD_DIM, DIL_BLOCK = 4, 128, 128
SSM_HEAD_DIM, SSM_GROUPS, SSM_STATE = 64, 2, 128
HGRN_EXPAND = 128

TOKEN_TILE = 512
SEQ_TILE = 512
COL_CHUNK = 512


def _params(*sem):
    return pltpu.CompilerParams(dimension_semantics=sem, vmem_limit_bytes=VMEM_LIMIT)


def _dot(a, b):
    return jnp.dot(a.astype(BF16), b.astype(BF16), preferred_element_type=F32)


def _dot_nt(a, b):
    return lax.dot_general(a.astype(BF16), b.astype(BF16), (((1,), (1,)), ((), ())),
                           preferred_element_type=F32)


def _dot_tn(a, b):
    return lax.dot_general(a.astype(BF16), b.astype(BF16), (((0,), (0,)), ((), ())),
                           preferred_element_type=F32)


def _split(x):
    hi = x.astype(BF16)
    return hi, (x - hi.astype(F32)).astype(BF16)


def _dot_exact_rhs(a, b01):
    hi, lo = _split(a)
    return (jnp.dot(hi, b01, preferred_element_type=F32)
            + jnp.dot(lo, b01, preferred_element_type=F32))


def _cumsum_rows(tri01, x):
    hi, lo = _split(x)
    return (jnp.dot(tri01, hi, preferred_element_type=F32)
            + jnp.dot(tri01, lo, preferred_element_type=F32))


def _dot_hilo(a, bh, bl):
    ah, al = _split(a)
    return (jnp.dot(ah, bh, preferred_element_type=F32)
            + jnp.dot(ah, bl, preferred_element_type=F32)
            + jnp.dot(al, bh, preferred_element_type=F32))


def _sigmoid(x):
    return 1.0 / (1.0 + jnp.exp(-x))


def _silu(x):
    return x * _sigmoid(x)


def _softplus(x):
    return jnp.maximum(x, 0.0) + jnp.log1p(jnp.exp(-jnp.abs(x)))


def _log_sigmoid(x):
    return -_softplus(-x)


def _tril01(n):
    r = lax.broadcasted_iota(jnp.int32, (n, n), 0)
    c = lax.broadcasted_iota(jnp.int32, (n, n), 1)
    return r >= c


def _norm_matmul_kernel(x_ref, g_ref, w_ref, *o_refs, segs):
    x = x_ref[...]
    h = x * lax.rsqrt(jnp.mean(x * x, axis=-1, keepdims=True) + EPS)
    h = (h * g_ref[...]).astype(BF16)
    c0 = 0
    for o_ref, width in zip(o_refs, segs):
        for j in range(0, width, COL_CHUNK):
            w = min(COL_CHUNK, width - j)
            o_ref[:, j:j + w] = jnp.dot(h, w_ref[:, c0 + j:c0 + j + w],
                                        preferred_element_type=F32).astype(o_ref.dtype)
        c0 += width


def _norm_matmul(x, g, w, segs, dtypes, tm=TOKEN_TILE):
    M, D = x.shape
    tm = min(tm, M)
    assert M % tm == 0 and w.shape == (D, sum(segs)) and all(s % LANE == 0 for s in segs)
    return pl.pallas_call(
        functools.partial(_norm_matmul_kernel, segs=tuple(segs)),
        grid=(M // tm,),
        in_specs=[pl.BlockSpec((tm, D), lambda i: (i, 0)),
                  pl.BlockSpec((1, D), lambda i: (0, 0)),
                  pl.BlockSpec(w.shape, lambda i: (0, 0))],
        out_specs=[pl.BlockSpec((tm, s), lambda i: (i, 0)) for s in segs],
        out_shape=[jax.ShapeDtypeStruct((M, s), dt) for s, dt in zip(segs, dtypes)],
        compiler_params=_params("parallel"),
        name="norm_matmul",
    )(x, g.reshape(1, D).astype(F32), w)


def _rms_rows(x_ref, g_ref):
    x = x_ref[...]
    h = x * lax.rsqrt(jnp.mean(x * x, axis=-1, keepdims=True) + EPS)
    return (h * g_ref[...]).astype(BF16)


def _col_chunks(width):
    return [(j, min(COL_CHUNK, width - j)) for j in range(0, width, COL_CHUNK)]


def _proj_store(h, w_ref, c0, o_ref, fn=None):
    width = o_ref.shape[-1]
    for j, w in _col_chunks(width):
        u = jnp.dot(h, w_ref[:, c0 + j:c0 + j + w], preferred_element_type=F32)
        if fn is not None:
            u = fn(u, slice(j, j + w))
        o_ref[:, j:j + w] = u.astype(o_ref.dtype)
    return c0 + width


def _gla_in_proj_kernel(x_ref, g_ref, w_ref, wgh_ref, wgl_ref, bg_ref,
                        q_ref, k_ref, v_ref, gate_ref, la_ref, xq_ref, *, qscale):
    h = _rms_rows(x_ref, g_ref)
    c = _proj_store(h, w_ref, 0, q_ref, lambda u, cs: u * qscale)
    c = _proj_store(h, w_ref, c, k_ref)
    c = _proj_store(h, w_ref, c, v_ref)
    c = _proj_store(h, w_ref, c, gate_ref, lambda u, cs: _silu(u))
    c = _proj_store(h, w_ref, c, xq_ref)
    glr = jnp.dot(h, w_ref[:, c:c + LANE], preferred_element_type=F32)
    for j, w in _col_chunks(la_ref.shape[-1]):
        cs = slice(j, j + w)
        pre = _dot_hilo(glr, wgh_ref[:, cs], wgl_ref[:, cs]) + bg_ref[:, cs]
        la_ref[:, cs] = _log_sigmoid(pre) * (1.0 / GLA_GATE_NORM)


def _hgrn_in_proj_kernel(x_ref, g_ref, w_ref, lb_ref, q_ref, k_ref, la_ref, v_ref, gate_ref, xq_ref):
    h = _rms_rows(x_ref, g_ref)
    c = _proj_store(h, w_ref, 0, q_ref, lambda u, cs: _silu(u))
    for j, w in _col_chunks(k_ref.shape[-1]):
        cs = slice(j, j + w)
        lb = lb_ref[:, cs]
        f = jnp.dot(h, w_ref[:, c + j:c + j + w], preferred_element_type=F32)
        fg = lb + (1.0 - lb) * _sigmoid(f)
        k_ref[:, cs] = 1.0 - fg
        la_ref[:, cs] = jnp.log(fg)
    c += k_ref.shape[-1]
    c = _proj_store(h, w_ref, c, v_ref)
    c = _proj_store(h, w_ref, c, gate_ref, lambda u, cs: _sigmoid(u))
    _proj_store(h, w_ref, c, xq_ref)


def _ssd_in_proj_kernel(x_ref, g_ref, w_ref, cw_ref, cb_ref, dtb_ref,
                        gate_ref, xc_ref, dt_ref, xq_ref, carry_ref):
    @pl.when(pl.program_id(1) == 0)
    def _():
        carry_ref[...] = jnp.zeros_like(carry_ref)

    T = x_ref.shape[0]
    h = _rms_rows(x_ref, g_ref)
    c = _proj_store(h, w_ref, 0, gate_ref, lambda u, cs: _silu(u))
    for j in range(0, xc_ref.shape[-1], 2 * LANE):
        cs = slice(j, j + 2 * LANE)
        u = jnp.dot(h, w_ref[:, c + j:c + j + 2 * LANE], preferred_element_type=F32)
        prev = carry_ref[:, cs]
        carry_ref[:, cs] = u[T - SUBLANE:T, :]
        xc_ref[:, cs] = _silu(_causal_taps(u, prev, cw_ref[:, cs], cb_ref[:, cs]))
    c += xc_ref.shape[-1]
    c = _proj_store(h, w_ref, c, dt_ref, lambda u, cs: _softplus(u + dtb_ref[:, cs]))
    _proj_store(h, w_ref, c, xq_ref)


def _in_proj_call(kern, x, g, w, consts, widths, dtypes, name, sequential=False, scratch=()):
    B, S, D = x.shape
    T = min(TOKEN_TILE, S)
    assert S % T == 0 and all(wd % LANE == 0 for wd in widths)
    return pl.pallas_call(
        kern,
        grid=(B, S // T),
        in_specs=[_seq_spec(T, D), _const_spec((1, D)),
                  pl.BlockSpec(w.shape, lambda b, t: (0, 0), pipeline_mode=pl.Buffered(1))]
                 + [_const_spec(a.shape) for a in consts],
        out_specs=[_seq_spec(T, wd) for wd in widths],
        out_shape=[jax.ShapeDtypeStruct((B, S, wd), dt) for wd, dt in zip(widths, dtypes)],
        scratch_shapes=list(scratch),
        compiler_params=_params("parallel", "arbitrary" if sequential else "parallel"),
        name=name,
    )(x, g.reshape(1, D).astype(F32), w, *consts)


def _head_rms_gate(o, g, gate, dv):
    y = o * lax.rsqrt(jnp.sum(o * o, axis=-1, keepdims=True) * (1.0 / dv) + EPS)
    return (y * g) * gate


GLA_GROUP = 4


def _gla_group(q_ref, k_ref, v_ref, gate_ref, la_ref, on, o_ref, st_ref, r0, NC, H, KP, VP, dv,
               tri01, tril):
    C = CHUNK
    rows = [slice(r0 + i * C, r0 + (i + 1) * C) for i in range(NC)]
    idx = [(i, h) for i in range(NC) for h in range(H)]
    bs = [_cumsum_rows(tri01, la_ref[r, :]) for r in rows]
    att, qi, ksb, dec = {}, {}, {}, {}
    for i in range(NC):
        q, k, b = q_ref[rows[i], :], k_ref[rows[i], :], bs[i]
        for h in range(H):
            ks = slice(h * KP, (h + 1) * KP)
            bh, bl, br = b[:, ks], b[C - 1:C, ks], b[C // 2 - 1:C // 2, ks]
            qe = q[:, ks] * jnp.exp(bh - br)
            ke = k[:, ks] * jnp.exp(br - bh)
            att[i, h] = _dot_nt(qe, ke)
            qi[i, h] = (qe * jnp.exp(br)).astype(BF16)
            ksb[i, h] = (ke * jnp.exp(bl - br)).astype(BF16)
            dec[i, h] = jnp.exp(bl)
    o_intra, upd = {}, {}
    for i, h in idx:
        vh = v_ref[rows[i], h * VP:(h + 1) * VP]
        o_intra[i, h] = _dot(jnp.where(tril, att[i, h], 0.0), vh)
        upd[i, h] = _dot_tn(vh, ksb[i, h])
    o_inter = {}
    for h in range(H):
        st = st_ref[h]
        for i in range(NC):
            o_inter[i, h] = _dot_nt(qi[i, h], st)
            st = dec[i, h] * st + upd[i, h]
        st_ref[h] = st
    for i, h in idx:
        vs = slice(h * VP, (h + 1) * VP)
        o = o_intra[i, h] + o_inter[i, h]
        o_ref[rows[i], vs] = _head_rms_gate(o, on, gate_ref[rows[i], vs], dv).astype(o_ref.dtype)


def _gla_kernel(q_ref, k_ref, v_ref, gate_ref, la_ref, on_ref, o_ref, st_ref, *, H, KP, VP, dv, T):
    @pl.when(pl.program_id(1) == 0)
    def _():
        st_ref[...] = jnp.zeros_like(st_ref)

    tril = _tril01(CHUNK)
    tri01 = tril.astype(BF16)
    on = on_ref[...]
    NC = GLA_GROUP
    for r0 in range(0, T, NC * CHUNK):
        _gla_group(q_ref, k_ref, v_ref, gate_ref, la_ref, on, o_ref, st_ref, r0, NC, H, KP, VP, dv,
                   tri01, tril)


def _seq_spec(T, width, col=0):
    return pl.BlockSpec((None, T, width), lambda b, t: (b, t, col))


def _const_spec(shape):
    return pl.BlockSpec(shape, lambda b, t: (0,) * len(shape))


def _gla_mixer(q, k, v, gate, la, on, *, H, KP, VP, dv, name):
    B, S, _ = q.shape
    T = min(SEQ_TILE, S)
    assert S % T == 0 and T % CHUNK == 0
    return pl.pallas_call(
        functools.partial(_gla_kernel, H=H, KP=KP, VP=VP, dv=dv, T=T),
        grid=(B, S // T),
        in_specs=[_seq_spec(T, H * KP), _seq_spec(T, H * KP), _seq_spec(T, H * VP),
                  _seq_spec(T, H * VP), _seq_spec(T, H * KP), _const_spec(on.shape)],
        out_specs=_seq_spec(T, H * VP),
        out_shape=jax.ShapeDtypeStruct((B, S, H * VP), BF16),
        scratch_shapes=[pltpu.VMEM((H, VP, KP), F32)],
        compiler_params=_params("parallel", "arbitrary"),
        name=name,
    )(q, k, v, gate, la, on)


def _ssd_kernel(gate_ref, xc_ref, dt_ref, alog_ref, dsk_ref, ng_ref, e_ref, o_ref, st_ref,
                *, T, DM, N, G):
    @pl.when(pl.program_id(1) == 0)
    def _():
        st_ref[...] = jnp.zeros_like(st_ref)

    C = CHUNK
    GW = DM // G
    tril = _tril01(C)
    tri01 = tril.astype(BF16)
    e01 = e_ref[...]
    a_neg = -jnp.exp(alog_ref[...])
    dsk = dsk_ref[...]
    ng = ng_ref[...]
    P = SSM_HEAD_DIM
    rr = lax.broadcasted_iota(jnp.int32, (C, DM), 0)
    cc = lax.broadcasted_iota(jnp.int32, (C, DM), 1)
    diag_sel = (cc & (P - 1)) == rr
    r2 = lax.broadcasted_iota(jnp.int32, (C, 2 * P), 0)
    c2 = lax.broadcasted_iota(jnp.int32, (C, 2 * P), 1)
    causal2 = r2 >= (c2 & (P - 1))
    lo_half = lax.broadcasted_iota(jnp.int32, (C, 2 * P), 1) < P

    def group(r0, NC):
        rows = [slice(r0 + i * C, r0 + (i + 1) * C) for i in range(NC)]
        xs, xdt, acs_e, acs_row, acs_last, bm, cm, cb2 = {}, {}, {}, {}, {}, {}, {}, {}
        for i in range(NC):
            xs[i] = xc_ref[rows[i], 0:DM]
            dt = dt_ref[rows[i], :]
            acs = _cumsum_rows(tri01, dt * a_neg)
            dt_e = _dot_exact_rhs(dt, e01)
            acs_e[i] = _dot_exact_rhs(acs, e01)
            acs_row[i] = jnp.sum(jnp.where(diag_sel, acs_e[i], 0.0), axis=0, keepdims=True)
            acs_last[i] = acs_e[i][C - 1:C, :]
            xdt[i] = xs[i] * dt_e
            for g in range(G):
                bm[i, g] = xc_ref[rows[i], DM + g * N:DM + (g + 1) * N]
                cm[i, g] = xc_ref[rows[i], DM + (G + g) * N:DM + (G + g + 1) * N]
                cb2[i, g] = _dot_nt(cm[i, g], jnp.concatenate([bm[i, g], bm[i, g]], axis=0))
        y_diag, upd = {}, {}
        for i in range(NC):
            for g in range(G):
                gs = slice(g * GW, (g + 1) * GW)
                yd = []
                for p in range(GW // (2 * P)):
                    ps = slice(g * GW + p * 2 * P, g * GW + (p + 1) * 2 * P)
                    seg = acs_e[i][:, ps] - acs_row[i][:, ps]
                    lmat = jnp.exp(jnp.where(causal2, seg, -jnp.inf))
                    xp = xdt[i][:, ps]
                    rhs = jnp.concatenate([jnp.where(lo_half, xp, 0.0),
                                           jnp.where(lo_half, 0.0, xp)], axis=0)
                    yd.append(_dot(cb2[i, g] * lmat, rhs))
                y_diag[i, g] = jnp.concatenate(yd, axis=1)
                xe = xdt[i][:, gs] * jnp.exp(acs_last[i][:, gs] - acs_e[i][:, gs])
                upd[i, g] = _dot_tn(bm[i, g], xe)
        y_off = {}
        for g in range(G):
            gs = slice(g * GW, (g + 1) * GW)
            st = st_ref[g]
            for i in range(NC):
                y_off[i, g] = _dot(cm[i, g], st)
                st = jnp.exp(acs_last[i][:, gs]) * st + upd[i, g]
            st_ref[g] = st
        for i in range(NC):
            for g in range(G):
                gs = slice(g * GW, (g + 1) * GW)
                y = y_diag[i, g] + y_off[i, g] * jnp.exp(acs_e[i][:, gs])
                y = y + dsk[:, gs] * xs[i][:, gs]
                y = y * gate_ref[rows[i], gs]
                y = y * lax.rsqrt(jnp.mean(y * y, axis=-1, keepdims=True) + EPS)
                o_ref[rows[i], gs] = (y * ng[:, gs]).astype(o_ref.dtype)

    for r0 in range(0, T, GLA_GROUP * C):
        group(r0, GLA_GROUP)


def _ssd_mixer(gate, xc, dt, alog, dsk, ng, e01):
    B, S, DM = gate.shape
    W = xc.shape[-1]
    T = min(SEQ_TILE, S)
    assert S % T == 0 and T % CHUNK == 0
    G, N = SSM_GROUPS, SSM_STATE
    return pl.pallas_call(
        functools.partial(_ssd_kernel, T=T, DM=DM, N=N, G=G),
        grid=(B, S // T),
        in_specs=[_seq_spec(T, DM), _seq_spec(T, W), _seq_spec(T, LANE),
                  _const_spec(alog.shape), _const_spec(dsk.shape), _const_spec(ng.shape),
                  _const_spec(e01.shape)],
        out_specs=_seq_spec(T, DM),
        out_shape=jax.ShapeDtypeStruct((B, S, DM), BF16),
        scratch_shapes=[pltpu.VMEM((G, N, DM // G), F32)],
        compiler_params=_params("parallel", "arbitrary"),
        name="ssd_mixer",
    )(gate, xc, dt, alog, dsk, ng, e01)


def _dil_in_proj_kernel(x_ref, g_ref, w_ref, posa_ref, posb_ref, qn_ref, kn_ref, invf_ref, sign_ref, *refs,
                        dils, NH, HD):
    NG = len(dils)
    GW = NH * HD
    outs, xq_ref, buf_ref = refs[:3 * NG], refs[3 * NG], refs[3 * NG + 1]
    tm = x_ref.shape[0]
    x = x_ref[...]
    h = x * lax.rsqrt(jnp.mean(x * x, axis=-1, keepdims=True) + EPS)
    h = (h * g_ref[...]).astype(BF16)
    lo = lax.broadcasted_iota(jnp.int32, (tm // 2, HD), 1) < HD // 2
    ang = jnp.where(lo, posa_ref[...], posb_ref[...]).astype(F32) * invf_ref[...]
    c2, s2 = jnp.cos(ang), jnp.sin(ang)
    c2r, s2r = pltpu.roll(c2, HD // 2, 1), pltpu.roll(s2, HD // 2, 1)
    cos = jnp.concatenate([jnp.where(lo, c2, c2r), jnp.where(lo, c2r, c2)], axis=0)
    sin = jnp.concatenate([jnp.where(lo, s2, s2r), jnp.where(lo, s2r, s2)], axis=0) * sign_ref[...]
    gains = (qn_ref[...], kn_ref[...])
    for kind in range(3):
        for g, r in enumerate(dils):
            c0 = (kind * NG + g) * GW
            u = jnp.dot(h, w_ref[:, c0:c0 + GW], preferred_element_type=F32)
            o_ref = outs[kind * NG + g]
            for hh in range(NH):
                hs = slice(hh * HD, (hh + 1) * HD)
                y = u[:, hs]
                if kind < 2:
                    y = (y * lax.rsqrt(jnp.mean(y * y, axis=-1, keepdims=True) + EPS)) * gains[kind]
                    y = y * cos + pltpu.roll(y, HD // 2, 1) * sin
                if r == 1:
                    o_ref[0, :, hs] = y.astype(o_ref.dtype)
                else:
                    slab = ((kind * NG + g) % 2) * NH + hh
                    buf_ref[slab] = y
                    for rho in range(r):
                        o_ref[rho, :, hs] = buf_ref[slab, pl.ds(rho, tm // r, stride=r), :].astype(
                            o_ref.dtype)
    xq_ref[...] = jnp.dot(h, w_ref[:, 3 * NG * GW:], preferred_element_type=F32)


def _dil_in_proj(x, g, w, pos, qn, kn, dils):
    B, S, D = x.shape
    NH, HD = DIL_HEADS, DIL_HEAD_DIM
    GW = NH * HD
    NG = len(dils)
    T = min(TOKEN_TILE, S)
    WQ = w.shape[1] - 3 * NG * GW
    assert S % T == 0 and all(T % (r * 2 * SUBLANE) == 0 for r in dils)
    half = HD // 2
    inv_freq = ROPE_THETA ** (-jnp.arange(half, dtype=F32) / half)
    invf = jnp.concatenate([inv_freq, inv_freq]).reshape(1, HD)
    sign = jnp.concatenate([-jnp.ones((half,), F32), jnp.ones((half,), F32)]).reshape(1, HD)
    grp_spec = [pl.BlockSpec((None, r, T // r, GW), lambda b, t: (b, 0, t, 0)) for r in dils]
    grp_shape = [jax.ShapeDtypeStruct((B, r, S // r, GW), BF16) for r in dils]
    res = pl.pallas_call(
        functools.partial(_dil_in_proj_kernel, dils=tuple(dils), NH=NH, HD=HD),
        grid=(B, S // T),
        in_specs=[_seq_spec(T, D), _const_spec((1, D)),
                  pl.BlockSpec(w.shape, lambda b, t: (0, 0), pipeline_mode=pl.Buffered(1)),
                  pl.BlockSpec((None, T // 2, 1), lambda b, t: (b, 2 * t, 0)),
                  pl.BlockSpec((None, T // 2, 1), lambda b, t: (b, 2 * t + 1, 0)),
                  _const_spec((1, HD)), _const_spec((1, HD)),
                  _const_spec((1, HD)), _const_spec((1, HD))],
        out_specs=grp_spec * 3 + [_seq_spec(T, WQ)],
        out_shape=grp_shape * 3 + [jax.ShapeDtypeStruct((B, S, WQ), F32)],
        scratch_shapes=[pltpu.VMEM((2 * NH, T, HD), F32)],
        compiler_params=_params("parallel", "parallel"),
        name="dilated_in_proj",
    )(x, g.reshape(1, D).astype(F32), w, pos, pos, qn.reshape(1, HD), kn.reshape(1, HD), invf, sign)
    return res[:NG], res[NG:2 * NG], res[2 * NG:3 * NG], res[3 * NG]


def _dil_attn_kernel(q_ref, kp_ref, kc_ref, vp_ref, vc_ref, o_ref, lse_ref, *, nq, W, NH, HD):
    n0 = pl.program_id(2)
    Q = DIL_BLOCK
    i_idx = lax.broadcasted_iota(jnp.int32, (Q, 2 * Q), 0)
    j_idx = lax.broadcasted_iota(jnp.int32, (Q, 2 * Q), 1)
    dist = Q + i_idx - j_idx
    band = (dist >= 0) & (dist <= W)
    jmin = jnp.where(n0 > 0, 0, Q)
    band_first = band & (j_idx >= jmin)
    lane = lax.broadcasted_iota(jnp.int32, (Q, LANE), 1)
    scale = HD ** -0.5
    idx = [(i, h) for i in range(nq) for h in range(NH)]
    hsl = lambda h: slice(h * HD, (h + 1) * HD)
    s = {}
    for i, h in idx:
        if i == 0:
            kk = jnp.concatenate([kp_ref[:, hsl(h)], kc_ref[0:Q, hsl(h)]], axis=0)
        else:
            kk = kc_ref[(i - 1) * Q:(i + 1) * Q, hsl(h)]
        s[i, h] = _dot_nt(q_ref[i * Q:(i + 1) * Q, hsl(h)], kk)
    p, m, l = {}, {}, {}
    for i, h in idx:
        sm = jnp.where(band_first if i == 0 else band, s[i, h] * scale, -jnp.inf)
        m[i, h] = jnp.max(sm, axis=-1, keepdims=True)
        p[i, h] = jnp.exp(sm - m[i, h])
        l[i, h] = jnp.sum(p[i, h], axis=-1, keepdims=True)
    for i, h in idx:
        if i == 0:
            vv = jnp.concatenate([vp_ref[:, hsl(h)], vc_ref[0:Q, hsl(h)]], axis=0)
        else:
            vv = vc_ref[(i - 1) * Q:(i + 1) * Q, hsl(h)]
        o_ref[i * Q:(i + 1) * Q, hsl(h)] = _dot(p[i, h], vv) * (1.0 / l[i, h])
    for i in range(nq):
        lse_tile = jnp.zeros((Q, LANE), F32)
        for h in range(NH):
            lse_tile = jnp.where(lane == h, m[i, h] + jnp.log(l[i, h]), lse_tile)
        lse_ref[i * Q:(i + 1) * Q, :] = lse_tile


def _dil_attention(q, k, v, window, dilation):
    B, r, L, GW = q.shape
    NH, HD, Q = DIL_HEADS, DIL_HEAD_DIM, DIL_BLOCK
    W = window // dilation
    assert r == dilation and L % Q == 0 and W <= Q
    nq = min(4, L // Q)
    assert (L // Q) % nq == 0
    cur = pl.BlockSpec((None, None, nq * Q, GW), lambda b, rho, n: (b, rho, n, 0))
    prev = pl.BlockSpec((None, None, Q, GW),
                        lambda b, rho, n: (b, rho, jnp.maximum(n * nq - 1, 0), 0))
    return pl.pallas_call(
        functools.partial(_dil_attn_kernel, nq=nq, W=W, NH=NH, HD=HD),
        grid=(B, r, L // (nq * Q)),
        in_specs=[cur, prev, cur, prev, cur],
        out_specs=[cur, pl.BlockSpec((None, None, nq * Q, LANE), lambda b, rho, n: (b, rho, n, 0))],
        out_shape=[jax.ShapeDtypeStruct((B, r, L, GW), F32),
                   jax.ShapeDtypeStruct((B, r, L, LANE), F32)],
        compiler_params=_params("parallel", "parallel", "arbitrary"),
        name=f"dilated_attention_r{r}",
    )(q, k, k, v, v)


def _dil_merge_tokens(o_refs, l_refs, obuf_ref, lbuf_ref, dils, NH, HD):
    tm = lbuf_ref.shape[1]
    ls = []
    for g, r in enumerate(dils):
        if r == 1:
            ls.append(l_refs[g][0])
            continue
        for rho in range(r):
            rows = pl.ds(rho, tm // r, stride=r)
            lbuf_ref[g, rows, :] = l_refs[g][rho]
            for h in range(NH):
                obuf_ref[g * NH + h, rows, :] = o_refs[g][rho, :, h * HD:(h + 1) * HD]
        ls.append(lbuf_ref[g])
    m = functools.reduce(jnp.maximum, ls)
    es = [jnp.exp(l - m) for l in ls]
    inv = 1.0 / functools.reduce(lambda a, b: a + b, es)
    heads = []
    for h in range(NH):
        hs = slice(h * HD, (h + 1) * HD)
        acc = None
        for g, r in enumerate(dils):
            o = o_refs[g][0, :, hs] if r == 1 else obuf_ref[g * NH + h]
            term = (es[g][:, h:h + 1] * inv[:, h:h + 1]) * o
            acc = term if acc is None else acc + term
        heads.append(acc.astype(BF16))
    return jnp.concatenate(heads, axis=1)


def _pair_rms(x, lo, hd):
    sq = x * x
    s_lo = jnp.sum(jnp.where(lo, sq, 0.0), axis=-1, keepdims=True)
    s_hi = jnp.sum(jnp.where(lo, 0.0, sq), axis=-1, keepdims=True)
    r = jnp.where(lo, lax.rsqrt(s_lo * (1.0 / hd) + EPS), lax.rsqrt(s_hi * (1.0 / hd) + EPS))
    return x * r


def _memory_xattn(xq_ref, kv_ref, qn2, kn2, hd):
    T, W = xq_ref.shape
    M = kv_ref.shape[0]
    assert 2 * hd == LANE and W % LANE == 0
    lo_q = lax.broadcasted_iota(jnp.int32, (T, LANE), 1) < hd
    lo_k = lax.broadcasted_iota(jnp.int32, (M, LANE), 1) < hd
    scale = hd ** -0.5
    units = [(p, first) for p in range(W // LANE) for first in (True, False)]
    s, vm = {}, {}
    for p in range(W // LANE):
        ps = slice(p * LANE, (p + 1) * LANE)
        q = _pair_rms(xq_ref[:, ps], lo_q, hd) * qn2
        k = (_pair_rms(kv_ref[:, ps], lo_k, hd) * kn2).astype(BF16)
        v = kv_ref[:, W + p * LANE:W + (p + 1) * LANE]
        for first in (True, False):
            qm = jnp.where(lo_q, q, 0.0) if first else jnp.where(lo_q, 0.0, q)
            vm[p, first] = jnp.where(lo_k, v, 0.0) if first else jnp.where(lo_k, 0.0, v)
            s[p, first] = _dot_nt(qm, k)
    e, l = {}, {}
    for u in units:
        sc = s[u] * scale
        e[u] = jnp.exp(sc - jnp.max(sc, axis=-1, keepdims=True))
        l[u] = jnp.sum(e[u], axis=-1, keepdims=True)
    t = {u: _dot(e[u], vm[u]) * (1.0 / l[u]) for u in units}
    outs = [t[p, True] + t[p, False] for p in range(W // LANE)]
    return jnp.concatenate(outs, axis=1).astype(BF16)


def _causal_taps(u, prev, cw, cb):
    K = cw.shape[0]
    sub = lax.broadcasted_iota(jnp.int32, (SUBLANE, u.shape[1]), 0)
    acc = cb + cw[K - 1:K, :] * u
    for k in range(1, K):
        rolled = pltpu.roll(u, k, 0)
        head = jnp.where(sub < k, pltpu.roll(prev, k, 0), rolled[0:SUBLANE, :])
        shifted = jnp.concatenate([head, rolled[SUBLANE:, :]], axis=0)
        acc = acc + cw[K - 1 - k:K - k, :] * shifted
    return acc


def _tail_kernel(x_ref, *refs, T, FF, FC, dils):
    n_tok = 1 if dils is None else 2 * len(dils)
    tok_refs, refs = refs[:n_tok], refs[n_tok:]
    (xq_ref, kv_ref, qn_ref, kn_ref, wt_ref, wx_ref, g_ref, wu_ref, cw_ref, cb_ref, wd_ref,
     o_ref, carry_ref, act_ref, x1_ref) = refs[:15]

    @pl.when(pl.program_id(1) == 0)
    def _():
        carry_ref[...] = jnp.zeros_like(carry_ref)

    D = o_ref.shape[1]
    xa = _memory_xattn(xq_ref, kv_ref, qn_ref[...], kn_ref[...], XA_HEAD_DIM)
    if dils is None:
        tok = tok_refs[0][...]
    else:
        tok = _dil_merge_tokens(tok_refs[:len(dils)], tok_refs[len(dils):], refs[15], refs[16],
                                dils, DIL_HEADS, DIL_HEAD_DIM)
    for n in range(0, D, COL_CHUNK):
        cs = slice(n, n + COL_CHUNK)
        x1_ref[:, cs] = (x_ref[:, cs] + jnp.dot(tok, wt_ref[:, cs], preferred_element_type=F32)
                         + jnp.dot(xa, wx_ref[:, cs], preferred_element_type=F32))

    x = x1_ref[...]
    h = x * lax.rsqrt(jnp.mean(x * x, axis=-1, keepdims=True) + EPS)
    h = (h * g_ref[...]).astype(BF16)

    def conv(cols):
        u = jnp.dot(h, wu_ref[:, cols], preferred_element_type=F32)
        prev = carry_ref[:, cols]
        carry_ref[:, cols] = u[T - SUBLANE:T, :]
        return _causal_taps(u, prev, cw_ref[:, cols], cb_ref[:, cols])

    for j in range(0, FF, FC):
        gate = conv(slice(j, j + FC))
        val = conv(slice(FF + j, FF + j + FC))
        act_ref[:, j:j + FC] = (_silu(gate) * val).astype(BF16)
    for n in range(0, D, COL_CHUNK):
        cs = slice(n, n + COL_CHUNK)
        o_ref[:, cs] = x1_ref[:, cs] + jnp.dot(act_ref[...], wd_ref[:, cs],
                                               preferred_element_type=F32)


def _layer_tail(x, tok, xq, kv_all, layer, qn2, kn2, wt, wx, g, wu, cw, cb, wd, dils=None):
    B, S, D = x.shape
    FF = wd.shape[0]
    WQ = xq.shape[-1]
    Mm = kv_all.shape[1]
    T = min(SEQ_TILE, S)
    FC = 256
    assert FF % FC == 0 and S % T == 0 and D % COL_CHUNK == 0
    resident = lambda a: pl.BlockSpec(a.shape, lambda b, t: (0, 0), pipeline_mode=pl.Buffered(1))
    scratch = [pltpu.VMEM((SUBLANE, 2 * FF), F32), pltpu.VMEM((T, FF), BF16),
               pltpu.VMEM((T, D), F32)]
    if dils is None:
        tok_args, tok_specs = [tok], [_seq_spec(T, tok.shape[-1])]
    else:
        outs, lses = tok
        tok_args = list(outs) + list(lses)
        tok_specs = [pl.BlockSpec((None, r, T // r, a.shape[-1]), lambda b, t: (b, 0, t, 0))
                     for a, r in zip(tok_args, list(dils) * 2)]
        scratch += [pltpu.VMEM((len(dils) * DIL_HEADS, T, DIL_HEAD_DIM), F32),
                    pltpu.VMEM((len(dils), T, LANE), F32)]
    return pl.pallas_call(
        functools.partial(_tail_kernel, T=T, FF=FF, FC=FC, dils=None if dils is None else tuple(dils)),
        grid=(B, S // T),
        in_specs=[_seq_spec(T, D)] + tok_specs + [
            _seq_spec(T, WQ), pl.BlockSpec((None, Mm, 2 * WQ), lambda b, t: (b, 0, layer)),
            _const_spec(qn2.shape), _const_spec(kn2.shape), resident(wt), resident(wx),
            _const_spec((1, D)), resident(wu), _const_spec(cw.shape), _const_spec(cb.shape),
            resident(wd)],
        out_specs=_seq_spec(T, D),
        out_shape=jax.ShapeDtypeStruct((B, S, D), F32),
        scratch_shapes=scratch,
        compiler_params=_params("parallel", "arbitrary"),
        name="layer_tail",
    )(x, *tok_args, xq, kv_all, qn2, kn2, wt, wx, g.reshape(1, D), wu, cw, cb, wd)


def _pad_heads_cols(w, H, d, dp):
    R = w.shape[0]
    return jnp.pad(w.reshape(R, H, d), ((0, 0), (0, 0), (0, dp - d))).reshape(R, H * dp)


def _pad_heads_rows(w, H, d, dp):
    C = w.shape[1]
    return jnp.pad(w.reshape(H, d, C), ((0, 0), (0, dp - d), (0, 0))).reshape(H * dp, C)


def _pad_cols(w, n):
    return jnp.pad(w, ((0, 0), (0, n - w.shape[1])))


def _cols(w, sizes):
    out, c = [], 0
    for s in sizes:
        out.append(w[:, c:c + s])
        c += s
    return out


def _out_weights(w_out, d_tok, tok_pad=None):
    wt, wx = w_out[:d_tok], w_out[d_tok:]
    if tok_pad is not None:
        wt = _pad_heads_rows(wt, *tok_pad)
    return wt.astype(BF16), wx.astype(BF16)


def _gla_layer(xf, norm_g, B, S, w_in, w_gate2, b_gate, o_norm):
    D = xf.shape[1]
    d_mix = 3 * D // 4
    H = GLA_HEADS
    dk, dv = d_mix // 2 // H, d_mix // H
    KP, VP = -(-dk // LANE) * LANE, -(-dv // LANE) * LANE
    q, k, v, glr, og, xq = _cols(w_in, [H * dk, H * dk, d_mix, GLA_RANK, d_mix, XA_HEADS * XA_HEAD_DIM])
    w = jnp.concatenate([_pad_heads_cols(q, H, dk, KP), _pad_heads_cols(k, H, dk, KP),
                         _pad_heads_cols(v, H, dv, VP), _pad_heads_cols(og, H, dv, VP),
                         xq, _pad_cols(glr, LANE)], axis=1).astype(BF16)
    wg = jnp.pad(_pad_heads_cols(w_gate2, H, dk, KP), ((0, LANE - GLA_RANK), (0, 0)))
    wgh, wgl = _split(wg)
    bg = _pad_heads_cols(b_gate.reshape(1, -1), H, dk, KP)
    on = _pad_cols(o_norm.reshape(1, -1), VP)
    q, k, v, gate, la, xq = _in_proj_call(
        functools.partial(_gla_in_proj_kernel, qscale=dk ** -0.5), xf.reshape(B, S, D), norm_g, w,
        [wgh, wgl, bg], [H * KP, H * KP, H * VP, H * VP, H * KP, XA_HEADS * XA_HEAD_DIM],
        [F32, F32, BF16, F32, F32, F32], "gla_in_proj")
    tok = _gla_mixer(q, k, v, gate, la, on, H=H, KP=KP, VP=VP, dv=dv, name="gla_mixer")
    return tok, xq, (H, dv, VP)


def _dilated_layer(xf, norm_g, B, S, w_in, q_norm, k_norm, positions):
    NG = len(DIL_GROUPS)
    GW = DIL_HEADS * DIL_HEAD_DIM
    dils = [d for _, d in DIL_GROUPS]
    q, k, v, xq = _cols(w_in, [NG * GW] * 3 + [XA_HEADS * XA_HEAD_DIM])
    w = jnp.concatenate([q, k, v, xq], axis=1).astype(BF16)
    qs, ks, vs, xq = _dil_in_proj(xf.reshape(B, S, -1), norm_g, w, positions.reshape(B, S, 1),
                                  q_norm, k_norm, dils)
    outs, lses = [], []
    for g, (window, dilation) in enumerate(DIL_GROUPS):
        o, lse = _dil_attention(qs[g], ks[g], vs[g], window, dilation)
        outs.append(o)
        lses.append(lse)
    return (outs, lses), xq, dils


def _mamba_layer(xf, norm_g, B, S, w_in, conv_w, conv_b, dt_bias, a_log, d_skip, norm_gain):
    D = xf.shape[1]
    d_mix = 3 * D // 4
    GN = SSM_GROUPS * SSM_STATE
    NHD = d_mix // SSM_HEAD_DIM
    z, xbc, dt, xq = _cols(w_in, [d_mix, d_mix + 2 * GN, NHD, XA_HEADS * XA_HEAD_DIM])
    w = jnp.concatenate([z, xbc, _pad_cols(dt, LANE), xq], axis=1).astype(BF16)
    padl = lambda a: _pad_cols(a.reshape(1, -1).astype(F32), LANE)
    W = d_mix + 2 * GN
    assert W % (2 * LANE) == 0
    gate, xc, dt, xq = _in_proj_call(
        _ssd_in_proj_kernel, xf.reshape(B, S, D), norm_g, w,
        [conv_w, conv_b.reshape(1, -1), padl(dt_bias)], [d_mix, W, LANE, XA_HEADS * XA_HEAD_DIM],
        [F32, F32, F32, F32], "ssd_in_proj", sequential=True,
        scratch=[pltpu.VMEM((SUBLANE, W), F32)])
    head = lax.broadcasted_iota(jnp.int32, (LANE, d_mix), 0)
    col = lax.broadcasted_iota(jnp.int32, (LANE, d_mix), 1)
    e01 = (col // SSM_HEAD_DIM == head).astype(BF16)
    tok = _ssd_mixer(gate, xc, dt, padl(a_log), jnp.repeat(d_skip, SSM_HEAD_DIM).reshape(1, -1),
                     norm_gain.reshape(1, -1), e01)
    return tok, xq, None


def _hgrn_layer(xf, norm_g, B, S, w_in, lower_bounds, o_norm, layer):
    D = xf.shape[1]
    d_mix = 3 * D // 4
    H = d_mix // HGRN_EXPAND
    KP, VP = HGRN_EXPAND, d_mix // H
    q, f, i, og, xq = _cols(w_in, [H * KP, H * KP, d_mix, d_mix, XA_HEADS * XA_HEAD_DIM])
    w = jnp.concatenate([q, f, i, og, xq], axis=1).astype(BF16)
    lbs = jnp.cumsum(jax.nn.softmax(lower_bounds.astype(F32), axis=0), axis=0)
    lb = (lbs[layer] - lbs[0]).reshape(1, -1)
    q, k, la, v, gate, xq = _in_proj_call(
        _hgrn_in_proj_kernel, xf.reshape(B, S, D), norm_g, w, [lb],
        [H * KP, H * KP, H * KP, d_mix, d_mix, XA_HEADS * XA_HEAD_DIM],
        [F32, F32, F32, BF16, F32, F32], "hgrn_in_proj")
    tok = _gla_mixer(q, k, v, gate, la, o_norm.reshape(1, -1), H=H, KP=KP, VP=VP, dv=VP,
                     name="hgrn_mixer")
    return tok, xq, None


def kernel(x, mem, positions, mem_norm, mix_norm, xa_w_kv, xa_q_norm, xa_k_norm, ffn_norm, ffn_w_up, ffn_conv_w, ffn_conv_b, ffn_w_down, a_w_in, a_w_gate2, a_b_gate, a_o_norm, a_w_out, b_w_in, b_q_norm, b_k_norm, b_w_out, c_w_in, c_conv_w, c_conv_b, c_dt_bias, c_a_log, c_d, c_norm, c_w_out, d_w_in, d_lower_bounds, d_o_norm, d_w_out):
    B, S, D = x.shape
    depth = mix_norm.shape[0]
    Mm = mem.shape[1]
    wkv = jnp.concatenate([xa_w_kv[i] for i in range(depth)], axis=1).astype(BF16)
    (kv_all,) = _norm_matmul(mem.reshape(B * Mm, D), mem_norm, wkv, [wkv.shape[1]], [F32])
    kv_all = kv_all.reshape(B, Mm, -1)
    pair = lambda g: jnp.concatenate([g, g]).reshape(1, -1)

    xf = x.reshape(B * S, D)
    for i in range(depth):
        kind = i % N_MIXERS
        if kind == 0:
            tok, xq, tok_pad = _gla_layer(xf, mix_norm[i], B, S, a_w_in, a_w_gate2, a_b_gate, a_o_norm)
            w_out, d_tok = a_w_out, 3 * D // 4
        elif kind == 1:
            tok, xq, dils = _dilated_layer(xf, mix_norm[i], B, S, b_w_in, b_q_norm, b_k_norm, positions)
            w_out, d_tok, tok_pad = b_w_out, DIL_HEADS * DIL_HEAD_DIM, None
        elif kind == 2:
            tok, xq, tok_pad = _mamba_layer(xf, mix_norm[i], B, S, c_w_in, c_conv_w, c_conv_b,
                                            c_dt_bias, c_a_log, c_d, c_norm)
            w_out, d_tok = c_w_out, 3 * D // 4
        else:
            tok, xq, tok_pad = _hgrn_layer(xf, mix_norm[i], B, S, d_w_in, d_lower_bounds, d_o_norm, i)
            w_out, d_tok = d_w_out, 3 * D // 4
        wt, wx = _out_weights(w_out, d_tok, tok_pad)
        xf = _layer_tail(xf.reshape(B, S, D), tok, xq, kv_all, i,
                         pair(xa_q_norm[i]), pair(xa_k_norm[i]), wt, wx, ffn_norm[i],
                         ffn_w_up[i].astype(BF16), ffn_conv_w[i], ffn_conv_b[i].reshape(1, -1),
                         ffn_w_down[i].astype(BF16),
                         dils=dils if kind == 1 else None).reshape(B * S, D)
    return xf.reshape(B, S, D)
```

```python
import functools
import math

import jax
import jax.numpy as jnp
from jax import lax
from jax.experimental import pallas as pl
from jax.experimental.pallas import tpu as pltpu

F32 = jnp.float32
BF16 = jnp.bfloat16

LANE = 128
SUBLANE = 8
VMEM_LIMIT = 56 * 1024 * 1024

EPS = 1e-6
ROPE_THETA = 10000.0
CHUNK = 64
N_MIXERS = 4
XA_HEADS, XA_HEAD_DIM = 4, 64
GLA_HEADS, GLA_RANK, GLA_GATE_NORM = 4, 16, 16.0
DIL_GROUPS = ((128, 1), (512, 4), (2048, 16))
DIL_HEADS, DIL_HEAD_DIM, DIL_BLOCK = 4, 128, 128
SSM_HEAD_DIM, SSM_GROUPS, SSM_STATE = 64, 2, 128
HGRN_EXPAND = 128

TOKEN_TILE = 1024
MIXER_TILE = 1024
SEQ_TILE = 512
COL_CHUNK = 512


def _params(*sem):
    return pltpu.CompilerParams(dimension_semantics=sem, vmem_limit_bytes=VMEM_LIMIT)


def _dot(a, b):
    return jnp.dot(a.astype(BF16), b.astype(BF16), preferred_element_type=F32)


def _dot_nt(a, b):
    return lax.dot_general(a.astype(BF16), b.astype(BF16), (((1,), (1,)), ((), ())),
                           preferred_element_type=F32)


def _dot_tn(a, b):
    return lax.dot_general(a.astype(BF16), b.astype(BF16), (((0,), (0,)), ((), ())),
                           preferred_element_type=F32)


def _split(x):
    hi = x.astype(BF16)
    return hi, (x - hi.astype(F32)).astype(BF16)


def _dot_exact_rhs(a, b01):
    hi, lo = _split(a)
    return (jnp.dot(hi, b01, preferred_element_type=F32)
            + jnp.dot(lo, b01, preferred_element_type=F32))


def _cumsum_rows(tri01, x):
    hi, lo = _split(x)
    return (jnp.dot(tri01, hi, preferred_element_type=F32)
            + jnp.dot(tri01, lo, preferred_element_type=F32))


def _dot_hilo(a, bh, bl):
    ah, al = _split(a)
    return (jnp.dot(ah, bh, preferred_element_type=F32)
            + jnp.dot(ah, bl, preferred_element_type=F32)
            + jnp.dot(al, bh, preferred_element_type=F32))


def _sigmoid(x):
    return 1.0 / (1.0 + jnp.exp(-x))


def _silu(x):
    return x * _sigmoid(x)


def _softplus(x):
    return jnp.maximum(x, 0.0) + jnp.log1p(jnp.exp(-jnp.abs(x)))


def _log_sigmoid(x):
    return -_softplus(-x)


def _tril01(n):
    r = lax.broadcasted_iota(jnp.int32, (n, n), 0)
    c = lax.broadcasted_iota(jnp.int32, (n, n), 1)
    return r >= c


def _norm_matmul_kernel(x_ref, g_ref, w_ref, *o_refs, segs):
    x = x_ref[...]
    h = x * lax.rsqrt(jnp.mean(x * x, axis=-1, keepdims=True) + EPS)
    h = (h * g_ref[...]).astype(BF16)
    c0 = 0
    for o_ref, width in zip(o_refs, segs):
        for j in range(0, width, COL_CHUNK):
            w = min(COL_CHUNK, width - j)
            o_ref[:, j:j + w] = jnp.dot(h, w_ref[:, c0 + j:c0 + j + w],
                                        preferred_element_type=F32).astype(o_ref.dtype)
        c0 += width


def _norm_matmul(x, g, w, segs, dtypes, tm=TOKEN_TILE):
    M, D = x.shape
    tm = min(tm, M)
    assert M % tm == 0 and w.shape == (D, sum(segs)) and all(s % LANE == 0 for s in segs)
    return pl.pallas_call(
        functools.partial(_norm_matmul_kernel, segs=tuple(segs)),
        grid=(M // tm,),
        in_specs=[pl.BlockSpec((tm, D), lambda i: (i, 0)),
                  pl.BlockSpec((1, D), lambda i: (0, 0)),
                  pl.BlockSpec(w.shape, lambda i: (0, 0))],
        out_specs=[pl.BlockSpec((tm, s), lambda i: (i, 0)) for s in segs],
        out_shape=[jax.ShapeDtypeStruct((M, s), dt) for s, dt in zip(segs, dtypes)],
        compiler_params=_params("parallel"),
        name="norm_matmul",
    )(x, g.reshape(1, D).astype(F32), w)


def _rms_rows(x_ref, g_ref):
    x = x_ref[...]
    h = x * lax.rsqrt(jnp.mean(x * x, axis=-1, keepdims=True) + EPS)
    return (h * g_ref[...]).astype(BF16)


def _col_chunks(width):
    return [(j, min(COL_CHUNK, width - j)) for j in range(0, width, COL_CHUNK)]


def _proj_store(h, w_ref, c0, o_ref, fn=None):
    width = o_ref.shape[-1]
    for j, w in _col_chunks(width):
        u = jnp.dot(h, w_ref[:, c0 + j:c0 + j + w], preferred_element_type=F32)
        if fn is not None:
            u = fn(u, slice(j, j + w))
        o_ref[:, j:j + w] = u.astype(o_ref.dtype)
    return c0 + width


def _gla_in_proj_kernel(x_ref, g_ref, w_ref, wgh_ref, wgl_ref, bg_ref,
                        q_ref, k_ref, v_ref, gate_ref, la_ref, xq_ref, *, qscale):
    h = _rms_rows(x_ref, g_ref)
    c = _proj_store(h, w_ref, 0, q_ref, lambda u, cs: u * qscale)
    c = _proj_store(h, w_ref, c, k_ref)
    c = _proj_store(h, w_ref, c, v_ref)
    c = _proj_store(h, w_ref, c, gate_ref, lambda u, cs: _silu(u))
    c = _proj_store(h, w_ref, c, xq_ref)
    glr = jnp.dot(h, w_ref[:, c:c + LANE], preferred_element_type=F32)
    for j, w in _col_chunks(la_ref.shape[-1]):
        cs = slice(j, j + w)
        pre = _dot_hilo(glr, wgh_ref[:, cs], wgl_ref[:, cs]) + bg_ref[:, cs]
        la_ref[:, cs] = _log_sigmoid(pre) * (1.0 / GLA_GATE_NORM)


def _hgrn_in_proj_kernel(x_ref, g_ref, w_ref, lb_ref, q_ref, k_ref, la_ref, v_ref, gate_ref, xq_ref):
    h = _rms_rows(x_ref, g_ref)
    c = _proj_store(h, w_ref, 0, q_ref, lambda u, cs: _silu(u))
    for j, w in _col_chunks(k_ref.shape[-1]):
        cs = slice(j, j + w)
        lb = lb_ref[:, cs]
        f = jnp.dot(h, w_ref[:, c + j:c + j + w], preferred_element_type=F32)
        fg = lb + (1.0 - lb) * _sigmoid(f)
        k_ref[:, cs] = 1.0 - fg
        la_ref[:, cs] = jnp.log(fg)
    c += k_ref.shape[-1]
    c = _proj_store(h, w_ref, c, v_ref)
    c = _proj_store(h, w_ref, c, gate_ref, lambda u, cs: _sigmoid(u))
    _proj_store(h, w_ref, c, xq_ref)


def _ssd_in_proj_kernel(x_ref, g_ref, w_ref, cw_ref, cb_ref, dtb_ref,
                        gate_ref, xc_ref, dt_ref, xq_ref, carry_ref):
    @pl.when(pl.program_id(1) == 0)
    def _():
        carry_ref[...] = jnp.zeros_like(carry_ref)

    T = x_ref.shape[0]
    h = _rms_rows(x_ref, g_ref)
    c = _proj_store(h, w_ref, 0, gate_ref, lambda u, cs: _silu(u))
    for j in range(0, xc_ref.shape[-1], 2 * LANE):
        cs = slice(j, j + 2 * LANE)
        u = jnp.dot(h, w_ref[:, c + j:c + j + 2 * LANE], preferred_element_type=F32)
        prev = carry_ref[:, cs]
        carry_ref[:, cs] = u[T - SUBLANE:T, :]
        xc_ref[:, cs] = _silu(_causal_taps(u, prev, cw_ref[:, cs], cb_ref[:, cs]))
    c += xc_ref.shape[-1]
    c = _proj_store(h, w_ref, c, dt_ref, lambda u, cs: _softplus(u + dtb_ref[:, cs]))
    _proj_store(h, w_ref, c, xq_ref)


def _in_proj_call(kern, x, g, w, consts, widths, dtypes, name, sequential=False, scratch=()):
    B, S, D = x.shape
    T = min(TOKEN_TILE, S)
    assert S % T == 0 and all(wd % LANE == 0 for wd in widths)
    return pl.pallas_call(
        kern,
        grid=(B, S // T),
        in_specs=[_seq_spec(T, D), _const_spec((1, D)),
                  pl.BlockSpec(w.shape, lambda b, t: (0, 0), pipeline_mode=pl.Buffered(1))]
                 + [_const_spec(a.shape) for a in consts],
        out_specs=[_seq_spec(T, wd) for wd in widths],
        out_shape=[jax.ShapeDtypeStruct((B, S, wd), dt) for wd, dt in zip(widths, dtypes)],
        scratch_shapes=list(scratch),
        compiler_params=_params("parallel", "arbitrary" if sequential else "parallel"),
        name=name,
    )(x, g.reshape(1, D).astype(F32), w, *consts)


def _head_rms_gate(o, g, gate, dv):
    y = o * lax.rsqrt(jnp.sum(o * o, axis=-1, keepdims=True) * (1.0 / dv) + EPS)
    return (y * g) * gate


GLA_GROUP = 4


def _gla_group(q_ref, k_ref, v_ref, gate_ref, la_ref, on, o_ref, st_ref, r0, NC, H, KP, VP, dv,
               tri01, tril):
    C = CHUNK
    rows = [slice(r0 + i * C, r0 + (i + 1) * C) for i in range(NC)]
    idx = [(i, h) for i in range(NC) for h in range(H)]
    bs = [_cumsum_rows(tri01, la_ref[r, :]) for r in rows]
    att, qi, ksb, dec = {}, {}, {}, {}
    for i in range(NC):
        q, k, b = q_ref[rows[i], :], k_ref[rows[i], :], bs[i]
        for h in range(H):
            ks = slice(h * KP, (h + 1) * KP)
            bh, bl, br = b[:, ks], b[C - 1:C, ks], b[C // 2 - 1:C // 2, ks]
            qe = q[:, ks] * jnp.exp(bh - br)
            ke = k[:, ks] * jnp.exp(br - bh)
            att[i, h] = _dot_nt(qe, ke)
            qi[i, h] = (qe * jnp.exp(br)).astype(BF16)
            ksb[i, h] = (ke * jnp.exp(bl - br)).astype(BF16)
            dec[i, h] = jnp.exp(bl)
    o_intra, upd = {}, {}
    for i, h in idx:
        vh = v_ref[rows[i], h * VP:(h + 1) * VP]
        o_intra[i, h] = _dot(jnp.where(tril, att[i, h], 0.0), vh)
        upd[i, h] = _dot_tn(vh, ksb[i, h])
    o_inter = {}
    for h in range(H):
        st = st_ref[h]
        for i in range(NC):
            o_inter[i, h] = _dot_nt(qi[i, h], st)
            st = dec[i, h] * st + upd[i, h]
        st_ref[h] = st
    for i, h in idx:
        vs = slice(h * VP, (h + 1) * VP)
        o = o_intra[i, h] + o_inter[i, h]
        o_ref[rows[i], vs] = _head_rms_gate(o, on, gate_ref[rows[i], vs], dv).astype(o_ref.dtype)


def _gla_kernel(q_ref, k_ref, v_ref, gate_ref, la_ref, on_ref, o_ref, st_ref, *, H, KP, VP, dv, T):
    @pl.when(pl.program_id(1) == 0)
    def _():
        st_ref[...] = jnp.zeros_like(st_ref)

    tril = _tril01(CHUNK)
    tri01 = tril.astype(BF16)
    on = on_ref[...]
    NC = GLA_GROUP
    for r0 in range(0, T, NC * CHUNK):
        _gla_group(q_ref, k_ref, v_ref, gate_ref, la_ref, on, o_ref, st_ref, r0, NC, H, KP, VP, dv,
                   tri01, tril)


def _seq_spec(T, width, col=0):
    return pl.BlockSpec((None, T, width), lambda b, t: (b, t, col))


def _const_spec(shape):
    return pl.BlockSpec(shape, lambda b, t: (0,) * len(shape))


def _gla_mixer(q, k, v, gate, la, on, *, H, KP, VP, dv, name):
    B, S, _ = q.shape
    T = min(MIXER_TILE, S)
    assert S % T == 0 and T % (GLA_GROUP * CHUNK) == 0
    return pl.pallas_call(
        functools.partial(_gla_kernel, H=H, KP=KP, VP=VP, dv=dv, T=T),
        grid=(B, S // T),
        in_specs=[_seq_spec(T, H * KP), _seq_spec(T, H * KP), _seq_spec(T, H * VP),
                  _seq_spec(T, H * VP), _seq_spec(T, H * KP), _const_spec(on.shape)],
        out_specs=_seq_spec(T, H * VP),
        out_shape=jax.ShapeDtypeStruct((B, S, H * VP), BF16),
        scratch_shapes=[pltpu.VMEM((H, VP, KP), F32)],
        compiler_params=_params("parallel", "arbitrary"),
        name=name,
    )(q, k, v, gate, la, on)


def _ssd_kernel(gate_ref, xc_ref, dt_ref, alog_ref, dsk_ref, ng_ref, e_ref, o_ref, st_ref,
                *, T, DM, N, G):
    @pl.when(pl.program_id(1) == 0)
    def _():
        st_ref[...] = jnp.zeros_like(st_ref)

    C = CHUNK
    GW = DM // G
    tril = _tril01(C)
    tri01 = tril.astype(BF16)
    e01 = e_ref[...]
    a_neg = -jnp.exp(alog_ref[...])
    dsk = dsk_ref[...]
    ng = ng_ref[...]
    P = SSM_HEAD_DIM
    rr = lax.broadcasted_iota(jnp.int32, (C, DM), 0)
    cc = lax.broadcasted_iota(jnp.int32, (C, DM), 1)
    diag_sel = (cc & (P - 1)) == rr
    r2 = lax.broadcasted_iota(jnp.int32, (C, 2 * P), 0)
    c2 = lax.broadcasted_iota(jnp.int32, (C, 2 * P), 1)
    causal2 = r2 >= (c2 & (P - 1))
    lo_half = lax.broadcasted_iota(jnp.int32, (C, 2 * P), 1) < P

    def group(r0, NC):
        rows = [slice(r0 + i * C, r0 + (i + 1) * C) for i in range(NC)]
        xs, xdt, acs_e, acs_row, acs_last, bm, cm, cb2 = {}, {}, {}, {}, {}, {}, {}, {}
        for i in range(NC):
            xs[i] = xc_ref[rows[i], 0:DM]
            dt = dt_ref[rows[i], :]
            acs = _cumsum_rows(tri01, dt * a_neg)
            dt_e = _dot_exact_rhs(dt, e01)
            acs_e[i] = _dot_exact_rhs(acs, e01)
            acs_row[i] = jnp.sum(jnp.where(diag_sel, acs_e[i], 0.0), axis=0, keepdims=True)
            acs_last[i] = acs_e[i][C - 1:C, :]
            xdt[i] = xs[i] * dt_e
            for g in range(G):
                bm[i, g] = xc_ref[rows[i], DM + g * N:DM + (g + 1) * N]
                cm[i, g] = xc_ref[rows[i], DM + (G + g) * N:DM + (G + g + 1) * N]
                cb2[i, g] = _dot_nt(cm[i, g], jnp.concatenate([bm[i, g], bm[i, g]], axis=0))
        y_diag, upd = {}, {}
        for i in range(NC):
            for g in range(G):
                gs = slice(g * GW, (g + 1) * GW)
                yd = []
                for p in range(GW // (2 * P)):
                    ps = slice(g * GW + p * 2 * P, g * GW + (p + 1) * 2 * P)
                    seg = acs_e[i][:, ps] - acs_row[i][:, ps]
                    lmat = jnp.exp(jnp.where(causal2, seg, -jnp.inf))
                    xp = xdt[i][:, ps]
                    rhs = jnp.concatenate([jnp.where(lo_half, xp, 0.0),
                                           jnp.where(lo_half, 0.0, xp)], axis=0)
                    yd.append(_dot(cb2[i, g] * lmat, rhs))
                y_diag[i, g] = jnp.concatenate(yd, axis=1)
                xe = xdt[i][:, gs] * jnp.exp(acs_last[i][:, gs] - acs_e[i][:, gs])
                upd[i, g] = _dot_tn(bm[i, g], xe)
        y_off = {}
        for g in range(G):
            gs = slice(g * GW, (g + 1) * GW)
            st = st_ref[g]
            for i in range(NC):
                y_off[i, g] = _dot(cm[i, g], st)
                st = jnp.exp(acs_last[i][:, gs]) * st + upd[i, g]
            st_ref[g] = st
        for i in range(NC):
            for g in range(G):
                gs = slice(g * GW, (g + 1) * GW)
                y = y_diag[i, g] + y_off[i, g] * jnp.exp(acs_e[i][:, gs])
                y = y + dsk[:, gs] * xs[i][:, gs]
                y = y * gate_ref[rows[i], gs]
                y = y * lax.rsqrt(jnp.mean(y * y, axis=-1, keepdims=True) + EPS)
                o_ref[rows[i], gs] = (y * ng[:, gs]).astype(o_ref.dtype)

    for r0 in range(0, T, GLA_GROUP * C):
        group(r0, GLA_GROUP)


def _ssd_mixer(gate, xc, dt, alog, dsk, ng, e01):
    B, S, DM = gate.shape
    W = xc.shape[-1]
    T = min(MIXER_TILE, S)
    assert S % T == 0 and T % (GLA_GROUP * CHUNK) == 0
    G, N = SSM_GROUPS, SSM_STATE
    return pl.pallas_call(
        functools.partial(_ssd_kernel, T=T, DM=DM, N=N, G=G),
        grid=(B, S // T),
        in_specs=[_seq_spec(T, DM), _seq_spec(T, W), _seq_spec(T, LANE),
                  _const_spec(alog.shape), _const_spec(dsk.shape), _const_spec(ng.shape),
                  _const_spec(e01.shape)],
        out_specs=_seq_spec(T, DM),
        out_shape=jax.ShapeDtypeStruct((B, S, DM), BF16),
        scratch_shapes=[pltpu.VMEM((G, N, DM // G), F32)],
        compiler_params=_params("parallel", "arbitrary"),
        name="ssd_mixer",
    )(gate, xc, dt, alog, dsk, ng, e01)


def _dil_in_proj_kernel(x_ref, g_ref, w_ref, posa_ref, posb_ref, qn_ref, kn_ref, invf_ref, sign_ref, *refs,
                        dils, NH, HD):
    NG = len(dils)
    GW = NH * HD
    outs, xq_ref, buf_ref = refs[:3 * NG], refs[3 * NG], refs[3 * NG + 1]
    tm = x_ref.shape[0]
    x = x_ref[...]
    h = x * lax.rsqrt(jnp.mean(x * x, axis=-1, keepdims=True) + EPS)
    h = (h * g_ref[...]).astype(BF16)
    lo = lax.broadcasted_iota(jnp.int32, (tm // 2, HD), 1) < HD // 2
    ang = jnp.where(lo, posa_ref[...], posb_ref[...]).astype(F32) * invf_ref[...]
    c2, s2 = jnp.cos(ang), jnp.sin(ang)
    c2r, s2r = pltpu.roll(c2, HD // 2, 1), pltpu.roll(s2, HD // 2, 1)
    cos = jnp.concatenate([jnp.where(lo, c2, c2r), jnp.where(lo, c2r, c2)], axis=0)
    sin = jnp.concatenate([jnp.where(lo, s2, s2r), jnp.where(lo, s2r, s2)], axis=0) * sign_ref[...]
    gains = (qn_ref[...], kn_ref[...])
    for kind in range(3):
        for g, r in enumerate(dils):
            c0 = (kind * NG + g) * GW
            u = jnp.dot(h, w_ref[:, c0:c0 + GW], preferred_element_type=F32)
            o_ref = outs[kind * NG + g]
            for hh in range(NH):
                hs = slice(hh * HD, (hh + 1) * HD)
                y = u[:, hs]
                if kind < 2:
                    y = (y * lax.rsqrt(jnp.mean(y * y, axis=-1, keepdims=True) + EPS)) * gains[kind]
                    y = y * cos + pltpu.roll(y, HD // 2, 1) * sin
                if r == 1:
                    o_ref[0, :, hs] = y.astype(o_ref.dtype)
                else:
                    slab = ((kind * NG + g) % 2) * NH + hh
                    buf_ref[slab] = y
                    for rho in range(r):
                        o_ref[rho, :, hs] = buf_ref[slab, pl.ds(rho, tm // r, stride=r), :].astype(
                            o_ref.dtype)
    xq_ref[...] = jnp.dot(h, w_ref[:, 3 * NG * GW:], preferred_element_type=F32)


def _dil_in_proj(x, g, w, pos, qn, kn, dils):
    B, S, D = x.shape
    NH, HD = DIL_HEADS, DIL_HEAD_DIM
    GW = NH * HD
    NG = len(dils)
    T = min(SEQ_TILE, S)
    WQ = w.shape[1] - 3 * NG * GW
    assert S % T == 0 and all(T % (r * 2 * SUBLANE) == 0 for r in dils)
    half = HD // 2
    inv_freq = ROPE_THETA ** (-jnp.arange(half, dtype=F32) / half)
    invf = jnp.concatenate([inv_freq, inv_freq]).reshape(1, HD)
    sign = jnp.concatenate([-jnp.ones((half,), F32), jnp.ones((half,), F32)]).reshape(1, HD)
    grp_spec = [pl.BlockSpec((None, r, T // r, GW), lambda b, t: (b, 0, t, 0)) for r in dils]
    grp_shape = [jax.ShapeDtypeStruct((B, r, S // r, GW), BF16) for r in dils]
    res = pl.pallas_call(
        functools.partial(_dil_in_proj_kernel, dils=tuple(dils), NH=NH, HD=HD),
        grid=(B, S // T),
        in_specs=[_seq_spec(T, D), _const_spec((1, D)),
                  pl.BlockSpec(w.shape, lambda b, t: (0, 0), pipeline_mode=pl.Buffered(1)),
                  pl.BlockSpec((None, T // 2, 1), lambda b, t: (b, 2 * t, 0)),
                  pl.BlockSpec((None, T // 2, 1), lambda b, t: (b, 2 * t + 1, 0)),
                  _const_spec((1, HD)), _const_spec((1, HD)),
                  _const_spec((1, HD)), _const_spec((1, HD))],
        out_specs=grp_spec * 3 + [_seq_spec(T, WQ)],
        out_shape=grp_shape * 3 + [jax.ShapeDtypeStruct((B, S, WQ), F32)],
        scratch_shapes=[pltpu.VMEM((2 * NH, T, HD), F32)],
        compiler_params=_params("parallel", "parallel"),
        name="dilated_in_proj",
    )(x, g.reshape(1, D).astype(F32), w, pos, pos, qn.reshape(1, HD), kn.reshape(1, HD), invf, sign)
    return res[:NG], res[NG:2 * NG], res[2 * NG:3 * NG], res[3 * NG]


def _dil_attn_kernel(q_ref, kp_ref, kc_ref, vp_ref, vc_ref, o_ref, lse_ref, *, nq, W, NH, HD):
    n0 = pl.program_id(2)
    Q = DIL_BLOCK
    i_idx = lax.broadcasted_iota(jnp.int32, (Q, 2 * Q), 0)
    j_idx = lax.broadcasted_iota(jnp.int32, (Q, 2 * Q), 1)
    dist = Q + i_idx - j_idx
    band = (dist >= 0) & (dist <= W)
    jmin = jnp.where(n0 > 0, 0, Q)
    band_first = band & (j_idx >= jmin)
    lane = lax.broadcasted_iota(jnp.int32, (Q, LANE), 1)
    scale = HD ** -0.5
    idx = [(i, h) for i in range(nq) for h in range(NH)]
    hsl = lambda h: slice(h * HD, (h + 1) * HD)
    s = {}
    for i, h in idx:
        if i == 0:
            kk = jnp.concatenate([kp_ref[:, hsl(h)], kc_ref[0:Q, hsl(h)]], axis=0)
        else:
            kk = kc_ref[(i - 1) * Q:(i + 1) * Q, hsl(h)]
        s[i, h] = _dot_nt(q_ref[i * Q:(i + 1) * Q, hsl(h)], kk)
    p, m, l = {}, {}, {}
    for i, h in idx:
        sm = jnp.where(band_first if i == 0 else band, s[i, h] * scale, -jnp.inf)
        m[i, h] = jnp.max(sm, axis=-1, keepdims=True)
        p[i, h] = jnp.exp(sm - m[i, h])
        l[i, h] = jnp.sum(p[i, h], axis=-1, keepdims=True)
    for i, h in idx:
        if i == 0:
            vv = jnp.concatenate([vp_ref[:, hsl(h)], vc_ref[0:Q, hsl(h)]], axis=0)
        else:
            vv = vc_ref[(i - 1) * Q:(i + 1) * Q, hsl(h)]
        o_ref[i * Q:(i + 1) * Q, hsl(h)] = _dot(p[i, h], vv) * (1.0 / l[i, h])
    for i in range(nq):
        lse_tile = jnp.zeros((Q, LANE), F32)
        for h in range(NH):
            lse_tile = jnp.where(lane == h, m[i, h] + jnp.log(l[i, h]), lse_tile)
        lse_ref[i * Q:(i + 1) * Q, :] = lse_tile


def _dil_attention(q, k, v, window, dilation):
    B, r, L, GW = q.shape
    NH, HD, Q = DIL_HEADS, DIL_HEAD_DIM, DIL_BLOCK
    W = window // dilation
    assert r == dilation and L % Q == 0 and W <= Q
    nq = min(4, L // Q)
    assert (L // Q) % nq == 0
    cur = pl.BlockSpec((None, None, nq * Q, GW), lambda b, rho, n: (b, rho, n, 0))
    prev = pl.BlockSpec((None, None, Q, GW),
                        lambda b, rho, n: (b, rho, jnp.maximum(n * nq - 1, 0), 0))
    return pl.pallas_call(
        functools.partial(_dil_attn_kernel, nq=nq, W=W, NH=NH, HD=HD),
        grid=(B, r, L // (nq * Q)),
        in_specs=[cur, prev, cur, prev, cur],
        out_specs=[cur, pl.BlockSpec((None, None, nq * Q, LANE), lambda b, rho, n: (b, rho, n, 0))],
        out_shape=[jax.ShapeDtypeStruct((B, r, L, GW), F32),
                   jax.ShapeDtypeStruct((B, r, L, LANE), F32)],
        compiler_params=_params("parallel", "parallel", "arbitrary"),
        name=f"dilated_attention_r{r}",
    )(q, k, k, v, v)


def _dil_merge_tokens(o_refs, l_refs, obuf_ref, lbuf_ref, dils, NH, HD):
    tm = lbuf_ref.shape[1]
    ls = []
    for g, r in enumerate(dils):
        if r == 1:
            ls.append(l_refs[g][0])
            continue
        for rho in range(r):
            rows = pl.ds(rho, tm // r, stride=r)
            lbuf_ref[g, rows, :] = l_refs[g][rho]
            for h in range(NH):
                obuf_ref[g * NH + h, rows, :] = o_refs[g][rho, :, h * HD:(h + 1) * HD]
        ls.append(lbuf_ref[g])
    m = functools.reduce(jnp.maximum, ls)
    es = [jnp.exp(l - m) for l in ls]
    inv = 1.0 / functools.reduce(lambda a, b: a + b, es)
    heads = []
    for h in range(NH):
        hs = slice(h * HD, (h + 1) * HD)
        acc = None
        for g, r in enumerate(dils):
            o = o_refs[g][0, :, hs] if r == 1 else obuf_ref[g * NH + h]
            term = (es[g][:, h:h + 1] * inv[:, h:h + 1]) * o
            acc = term if acc is None else acc + term
        heads.append(acc.astype(BF16))
    return jnp.concatenate(heads, axis=1)


def _pair_rms(x, lo, hd):
    sq = x * x
    s_lo = jnp.sum(jnp.where(lo, sq, 0.0), axis=-1, keepdims=True)
    s_hi = jnp.sum(jnp.where(lo, 0.0, sq), axis=-1, keepdims=True)
    r = jnp.where(lo, lax.rsqrt(s_lo * (1.0 / hd) + EPS), lax.rsqrt(s_hi * (1.0 / hd) + EPS))
    return x * r


def _memory_xattn(xq_ref, kv_ref, qn2, kn2, hd):
    T, W = xq_ref.shape
    M = kv_ref.shape[0]
    assert 2 * hd == LANE and W % LANE == 0
    lo_q = lax.broadcasted_iota(jnp.int32, (T, LANE), 1) < hd
    lo_k = lax.broadcasted_iota(jnp.int32, (M, LANE), 1) < hd
    scale = hd ** -0.5
    units = [(p, first) for p in range(W // LANE) for first in (True, False)]
    s, vm = {}, {}
    for p in range(W // LANE):
        ps = slice(p * LANE, (p + 1) * LANE)
        q = _pair_rms(xq_ref[:, ps], lo_q, hd) * qn2
        k = (_pair_rms(kv_ref[:, ps], lo_k, hd) * kn2).astype(BF16)
        v = kv_ref[:, W + p * LANE:W + (p + 1) * LANE]
        for first in (True, False):
            qm = jnp.where(lo_q, q, 0.0) if first else jnp.where(lo_q, 0.0, q)
            vm[p, first] = jnp.where(lo_k, v, 0.0) if first else jnp.where(lo_k, 0.0, v)
            s[p, first] = _dot_nt(qm, k)
    e, l = {}, {}
    for u in units:
        sc = s[u] * scale
        e[u] = jnp.exp(sc - jnp.max(sc, axis=-1, keepdims=True))
        l[u] = jnp.sum(e[u], axis=-1, keepdims=True)
    t = {u: _dot(e[u], vm[u]) * (1.0 / l[u]) for u in units}
    outs = [t[p, True] + t[p, False] for p in range(W // LANE)]
    return jnp.concatenate(outs, axis=1).astype(BF16)


def _causal_taps(u, prev, cw, cb):
    K = cw.shape[0]
    sub = lax.broadcasted_iota(jnp.int32, (SUBLANE, u.shape[1]), 0)
    acc = cb + cw[K - 1:K, :] * u
    for k in range(1, K):
        rolled = pltpu.roll(u, k, 0)
        head = jnp.where(sub < k, pltpu.roll(prev, k, 0), rolled[0:SUBLANE, :])
        shifted = jnp.concatenate([head, rolled[SUBLANE:, :]], axis=0)
        acc = acc + cw[K - 1 - k:K - k, :] * shifted
    return acc


def _tail_kernel(x_ref, *refs, T, FF, FC, dils):
    n_tok = 1 if dils is None else 2 * len(dils)
    tok_refs, refs = refs[:n_tok], refs[n_tok:]
    (xq_ref, kv_ref, qn_ref, kn_ref, wt_ref, wx_ref, g_ref, wu_ref, cw_ref, cb_ref, wd_ref,
     o_ref, carry_ref, act_ref, x1_ref) = refs[:15]

    @pl.when(pl.program_id(1) == 0)
    def _():
        carry_ref[...] = jnp.zeros_like(carry_ref)

    D = o_ref.shape[1]
    xa = _memory_xattn(xq_ref, kv_ref, qn_ref[...], kn_ref[...], XA_HEAD_DIM)
    if dils is None:
        tok = tok_refs[0][...]
    else:
        tok = _dil_merge_tokens(tok_refs[:len(dils)], tok_refs[len(dils):], refs[15], refs[16],
                                dils, DIL_HEADS, DIL_HEAD_DIM)
    for n in range(0, D, COL_CHUNK):
        cs = slice(n, n + COL_CHUNK)
        x1_ref[:, cs] = (x_ref[:, cs] + jnp.dot(tok, wt_ref[:, cs], preferred_element_type=F32)
                         + jnp.dot(xa, wx_ref[:, cs], preferred_element_type=F32))

    x = x1_ref[...]
    h = x * lax.rsqrt(jnp.mean(x * x, axis=-1, keepdims=True) + EPS)
    h = (h * g_ref[...]).astype(BF16)

    def conv(cols):
        u = jnp.dot(h, wu_ref[:, cols], preferred_element_type=F32)
        prev = carry_ref[:, cols]
        carry_ref[:, cols] = u[T - SUBLANE:T, :]
        return _causal_taps(u, prev, cw_ref[:, cols], cb_ref[:, cols])

    for j in range(0, FF, FC):
        gate = conv(slice(j, j + FC))
        val = conv(slice(FF + j, FF + j + FC))
        act_ref[:, j:j + FC] = (_silu(gate) * val).astype(BF16)
    for n in range(0, D, COL_CHUNK):
        cs = slice(n, n + COL_CHUNK)
        o_ref[:, cs] = x1_ref[:, cs] + jnp.dot(act_ref[...], wd_ref[:, cs],
                                               preferred_element_type=F32)


def _layer_tail(x, tok, xq, kv_all, layer, qn2, kn2, wt, wx, g, wu, cw, cb, wd, dils=None):
    B, S, D = x.shape
    FF = wd.shape[0]
    WQ = xq.shape[-1]
    Mm = kv_all.shape[1]
    T = min(SEQ_TILE, S)
    FC = 256
    assert FF % FC == 0 and S % T == 0 and D % COL_CHUNK == 0
    resident = lambda a: pl.BlockSpec(a.shape, lambda b, t: (0, 0), pipeline_mode=pl.Buffered(1))
    scratch = [pltpu.VMEM((SUBLANE, 2 * FF), F32), pltpu.VMEM((T, FF), BF16),
               pltpu.VMEM((T, D), F32)]
    if dils is None:
        tok_args, tok_specs = [tok], [_seq_spec(T, tok.shape[-1])]
    else:
        outs, lses = tok
        tok_args = list(outs) + list(lses)
        tok_specs = [pl.BlockSpec((None, r, T // r, a.shape[-1]), lambda b, t: (b, 0, t, 0))
                     for a, r in zip(tok_args, list(dils) * 2)]
        scratch += [pltpu.VMEM((len(dils) * DIL_HEADS, T, DIL_HEAD_DIM), F32),
                    pltpu.VMEM((len(dils), T, LANE), F32)]
    return pl.pallas_call(
        functools.partial(_tail_kernel, T=T, FF=FF, FC=FC, dils=None if dils is None else tuple(dils)),
        grid=(B, S // T),
        in_specs=[_seq_spec(T, D)] + tok_specs + [
            _seq_spec(T, WQ), pl.BlockSpec((None, Mm, 2 * WQ), lambda b, t: (b, 0, layer)),
            _const_spec(qn2.shape), _const_spec(kn2.shape), resident(wt), resident(wx),
            _const_spec((1, D)), resident(wu), _const_spec(cw.shape), _const_spec(cb.shape),
            resident(wd)],
        out_specs=_seq_spec(T, D),
        out_shape=jax.ShapeDtypeStruct((B, S, D), F32),
        scratch_shapes=scratch,
        compiler_params=_params("parallel", "arbitrary"),
        name="layer_tail",
    )(x, *tok_args, xq, kv_all, qn2, kn2, wt, wx, g.reshape(1, D), wu, cw, cb, wd)


def _pad_heads_cols(w, H, d, dp):
    R = w.shape[0]
    return jnp.pad(w.reshape(R, H, d), ((0, 0), (0, 0), (0, dp - d))).reshape(R, H * dp)


def _pad_heads_rows(w, H, d, dp):
    C = w.shape[1]
    return jnp.pad(w.reshape(H, d, C), ((0, 0), (0, dp - d), (0, 0))).reshape(H * dp, C)


def _pad_cols(w, n):
    return jnp.pad(w, ((0, 0), (0, n - w.shape[1])))


def _cols(w, sizes):
    out, c = [], 0
    for s in sizes:
        out.append(w[:, c:c + s])
        c += s
    return out


def _out_weights(w_out, d_tok, tok_pad=None):
    wt, wx = w_out[:d_tok], w_out[d_tok:]
    if tok_pad is not None:
        wt = _pad_heads_rows(wt, *tok_pad)
    return wt.astype(BF16), wx.astype(BF16)


def _gla_layer(xf, norm_g, B, S, w_in, w_gate2, b_gate, o_norm):
    D = xf.shape[1]
    d_mix = 3 * D // 4
    H = GLA_HEADS
    dk, dv = d_mix // 2 // H, d_mix // H
    KP, VP = -(-dk // LANE) * LANE, -(-dv // LANE) * LANE
    q, k, v, glr, og, xq = _cols(w_in, [H * dk, H * dk, d_mix, GLA_RANK, d_mix, XA_HEADS * XA_HEAD_DIM])
    w = jnp.concatenate([_pad_heads_cols(q, H, dk, KP), _pad_heads_cols(k, H, dk, KP),
                         _pad_heads_cols(v, H, dv, VP), _pad_heads_cols(og, H, dv, VP),
                         xq, _pad_cols(glr, LANE)], axis=1).astype(BF16)
    wg = jnp.pad(_pad_heads_cols(w_gate2, H, dk, KP), ((0, LANE - GLA_RANK), (0, 0)))
    wgh, wgl = _split(wg)
    bg = _pad_heads_cols(b_gate.reshape(1, -1), H, dk, KP)
    on = _pad_cols(o_norm.reshape(1, -1), VP)
    q, k, v, gate, la, xq = _in_proj_call(
        functools.partial(_gla_in_proj_kernel, qscale=dk ** -0.5), xf.reshape(B, S, D), norm_g, w,
        [wgh, wgl, bg], [H * KP, H * KP, H * VP, H * VP, H * KP, XA_HEADS * XA_HEAD_DIM],
        [F32, F32, BF16, F32, F32, F32], "gla_in_proj")
    tok = _gla_mixer(q, k, v, gate, la, on, H=H, KP=KP, VP=VP, dv=dv, name="gla_mixer")
    return tok, xq, (H, dv, VP)


def _dilated_layer(xf, norm_g, B, S, w_in, q_norm, k_norm, positions):
    NG = len(DIL_GROUPS)
    GW = DIL_HEADS * DIL_HEAD_DIM
    dils = [d for _, d in DIL_GROUPS]
    q, k, v, xq = _cols(w_in, [NG * GW] * 3 + [XA_HEADS * XA_HEAD_DIM])
    w = jnp.concatenate([q, k, v, xq], axis=1).astype(BF16)
    qs, ks, vs, xq = _dil_in_proj(xf.reshape(B, S, -1), norm_g, w, positions.reshape(B, S, 1),
                                  q_norm, k_norm, dils)
    outs, lses = [], []
    for g, (window, dilation) in enumerate(DIL_GROUPS):
        o, lse = _dil_attention(qs[g], ks[g], vs[g], window, dilation)
        outs.append(o)
        lses.append(lse)
    return (outs, lses), xq, dils


def _mamba_layer(xf, norm_g, B, S, w_in, conv_w, conv_b, dt_bias, a_log, d_skip, norm_gain):
    D = xf.shape[1]
    d_mix = 3 * D // 4
    GN = SSM_GROUPS * SSM_STATE
    NHD = d_mix // SSM_HEAD_DIM
    z, xbc, dt, xq = _cols(w_in, [d_mix, d_mix + 2 * GN, NHD, XA_HEADS * XA_HEAD_DIM])
    w = jnp.concatenate([z, xbc, _pad_cols(dt, LANE), xq], axis=1).astype(BF16)
    padl = lambda a: _pad_cols(a.reshape(1, -1).astype(F32), LANE)
    W = d_mix + 2 * GN
    assert W % (2 * LANE) == 0
    gate, xc, dt, xq = _in_proj_call(
        _ssd_in_proj_kernel, xf.reshape(B, S, D), norm_g, w,
        [conv_w, conv_b.reshape(1, -1), padl(dt_bias)], [d_mix, W, LANE, XA_HEADS * XA_HEAD_DIM],
        [F32, F32, F32, F32], "ssd_in_proj", sequential=True,
        scratch=[pltpu.VMEM((SUBLANE, W), F32)])
    head = lax.broadcasted_iota(jnp.int32, (LANE, d_mix), 0)
    col = lax.broadcasted_iota(jnp.int32, (LANE, d_mix), 1)
    e01 = (col // SSM_HEAD_DIM == head).astype(BF16)
    tok = _ssd_mixer(gate, xc, dt, padl(a_log), jnp.repeat(d_skip, SSM_HEAD_DIM).reshape(1, -1),
                     norm_gain.reshape(1, -1), e01)
    return tok, xq, None


def _hgrn_layer(xf, norm_g, B, S, w_in, lower_bounds, o_norm, layer):
    D = xf.shape[1]
    d_mix = 3 * D // 4
    H = d_mix // HGRN_EXPAND
    KP, VP = HGRN_EXPAND, d_mix // H
    q, f, i, og, xq = _cols(w_in, [H * KP, H * KP, d_mix, d_mix, XA_HEADS * XA_HEAD_DIM])
    w = jnp.concatenate([q, f, i, og, xq], axis=1).astype(BF16)
    lbs = jnp.cumsum(jax.nn.softmax(lower_bounds.astype(F32), axis=0), axis=0)
    lb = (lbs[layer] - lbs[0]).reshape(1, -1)
    q, k, la, v, gate, xq = _in_proj_call(
        _hgrn_in_proj_kernel, xf.reshape(B, S, D), norm_g, w, [lb],
        [H * KP, H * KP, H * KP, d_mix, d_mix, XA_HEADS * XA_HEAD_DIM],
        [F32, F32, F32, BF16, F32, F32], "hgrn_in_proj")
    tok = _gla_mixer(q, k, v, gate, la, o_norm.reshape(1, -1), H=H, KP=KP, VP=VP, dv=VP,
                     name="hgrn_mixer")
    return tok, xq, None


def kernel(x, mem, positions, mem_norm, mix_norm, xa_w_kv, xa_q_norm, xa_k_norm, ffn_norm, ffn_w_up, ffn_conv_w, ffn_conv_b, ffn_w_down, a_w_in, a_w_gate2, a_b_gate, a_o_norm, a_w_out, b_w_in, b_q_norm, b_k_norm, b_w_out, c_w_in, c_conv_w, c_conv_b, c_dt_bias, c_a_log, c_d, c_norm, c_w_out, d_w_in, d_lower_bounds, d_o_norm, d_w_out):
    B, S, D = x.shape
    depth = mix_norm.shape[0]
    Mm = mem.shape[1]
    wkv = jnp.concatenate([xa_w_kv[i] for i in range(depth)], axis=1).astype(BF16)
    (kv_all,) = _norm_matmul(mem.reshape(B * Mm, D), mem_norm, wkv, [wkv.shape[1]], [F32])
    kv_all = kv_all.reshape(B, Mm, -1)
    pair = lambda g: jnp.concatenate([g, g]).reshape(1, -1)

    xf = x.reshape(B * S, D)
    for i in range(depth):
        kind = i % N_MIXERS
        if kind == 0:
            tok, xq, tok_pad = _gla_layer(xf, mix_norm[i], B, S, a_w_in, a_w_gate2, a_b_gate, a_o_norm)
            w_out, d_tok = a_w_out, 3 * D // 4
        elif kind == 1:
            tok, xq, dils = _dilated_layer(xf, mix_norm[i], B, S, b_w_in, b_q_norm, b_k_norm, positions)
            w_out, d_tok, tok_pad = b_w_out, DIL_HEADS * DIL_HEAD_DIM, None
        elif kind == 2:
            tok, xq, tok_pad = _mamba_layer(xf, mix_norm[i], B, S, c_w_in, c_conv_w, c_conv_b,
                                            c_dt_bias, c_a_log, c_d, c_norm)
            w_out, d_tok = c_w_out, 3 * D // 4
        else:
            tok, xq, tok_pad = _hgrn_layer(xf, mix_norm[i], B, S, d_w_in, d_lower_bounds, d_o_norm, i)
            w_out, d_tok = d_w_out, 3 * D // 4
        wt, wx = _out_weights(w_out, d_tok, tok_pad)
        xf = _layer_tail(xf.reshape(B, S, D), tok, xq, kv_all, i,
                         pair(xa_q_norm[i]), pair(xa_k_norm[i]), wt, wx, ffn_norm[i],
                         ffn_w_up[i].astype(BF16), ffn_conv_w[i], ffn_conv_b[i].reshape(1, -1),
                         ffn_w_down[i].astype(BF16),
                         dils=dils if kind == 1 else None).reshape(B * S, D)
    return xf.reshape(B, S, D)
```
